```python
import jax, jax.numpy as jnp
from jax import lax
import numpy as np

D_MODEL = 1024
BATCH = 16
SEQ = 2048
DEPTH = 2
DEC_BATCH = 128
DEC_SEQ = 1
PAST_LEN = 8192
PAGE_SIZE = 128

SB_HEADS = 4
SB_HEAD_DIM = 64
SB_WIDTH = SB_HEADS * SB_HEAD_DIM
MLA_HEADS = 4
MLA_NOPE = 64
MLA_ROPE = 32
MLA_V = 64
MLA_Q_RANK = 384
MLA_KV_RANK = 256
MLA_SCALE = (MLA_NOPE + MLA_ROPE) ** -0.5
ROPE_THETA = 10000.0
SSM_HEADS = 8
SSM_HEAD_DIM = 64
SSM_INNER = SSM_HEADS * SSM_HEAD_DIM
SSM_GROUPS = 2
SSM_STATE = 128
CONV_WIDTH = 4
CONV_DIM = SSM_INNER + 2 * SSM_GROUPS * SSM_STATE
SSD_CHUNK = 128
MIX_WIDTH = SB_WIDTH + MLA_HEADS * MLA_V + SSM_INNER
IN_SIZES = (SB_WIDTH, SB_WIDTH, SB_WIDTH, MLA_Q_RANK, MLA_KV_RANK, MLA_ROPE, SSM_INNER, CONV_DIM, SSM_HEADS)
IN_WIDTH = 3 * SB_WIDTH + MLA_Q_RANK + MLA_KV_RANK + MLA_ROPE + SSM_INNER + CONV_DIM + SSM_HEADS
FFN_HIDDEN = -(-8 * D_MODEL // (3 * 256)) * 256
Q_BLOCK = 128
RMS_EPS = 1e-6

kernel_name = "hymba_sb_mla_ssd_decode_step"

F32 = jnp.float32


def rmsnorm(x, w):
    xf = x.astype(F32)
    y = xf * lax.rsqrt(jnp.mean(xf * xf, axis=-1, keepdims=True) + RMS_EPS)
    return (y * w.astype(F32)).astype(x.dtype)


def gated_group_rmsnorm(y, z, w):
    g = y.astype(F32) * jax.nn.silu(z.astype(F32))
    shp = g.shape
    g = g.reshape(*shp[:-1], SSM_GROUPS, shp[-1] // SSM_GROUPS)
    g = g * lax.rsqrt(jnp.mean(g * g, axis=-1, keepdims=True) + RMS_EPS)
    return (g.reshape(shp) * w.astype(F32)).astype(z.dtype)


def rope(x, pos):
    half = x.shape[-1] // 2
    freqs = ROPE_THETA ** (-jnp.arange(half, dtype=F32) / half)
    ang = pos.astype(F32)[:, None] * freqs
    ang = ang.reshape(ang.shape[0], *([1] * (x.ndim - 3)), half)
    c, s = jnp.cos(ang), jnp.sin(ang)
    xf = x.astype(F32)
    x1, x2 = xf[..., :half], xf[..., half:]
    return jnp.concatenate([x1 * c - x2 * s, x1 * s + x2 * c], axis=-1).astype(x.dtype)


def sb_attend(q, q_pos, k, v, k_pos):
    z = jnp.einsum('bqhd,bshd->bhqs', q.astype(F32), k.astype(F32)) * (SB_HEAD_DIM ** -0.5)
    valid = k_pos[None, :] < q_pos[:, None]
    log_keep = jnp.where(valid, jax.nn.log_sigmoid(-z), 0.0)
    rc = lax.cumsum(log_keep, axis=3, reverse=True)
    after = jnp.concatenate([rc[..., 1:], jnp.zeros_like(rc[..., :1])], axis=-1)
    w = jnp.where(valid, jnp.exp(jax.nn.log_sigmoid(z) + after), 0.0)
    return jnp.einsum('bhqs,bshd->bqhd', w, v.astype(F32)).astype(q.dtype)


def mla_attend(q_lat, q_pe, q_pos, ckv, kpe, k_pos):
    s = (jnp.einsum('bqhr,bsr->bhqs', q_lat.astype(F32), ckv.astype(F32))
         + jnp.einsum('bqhe,bse->bhqs', q_pe.astype(F32), kpe.astype(F32))) * MLA_SCALE
    s = jnp.where(k_pos[None, :] <= q_pos[:, None], s, -jnp.inf)
    p = jax.nn.softmax(s, axis=-1)
    return jnp.einsum('bhqs,bsr->bqhr', p, ckv.astype(F32)).astype(q_lat.dtype)


def query_blocks(fn, qs, q_pos):
    b, S = qs[0].shape[:2]
    nb = S // Q_BLOCK
    blk = tuple(jnp.moveaxis(q.reshape(b, nb, Q_BLOCK, *q.shape[2:]), 1, 0) for q in qs)
    out = lax.map(lambda a: fn(*a), blk + (q_pos.reshape(nb, Q_BLOCK),))
    return jnp.moveaxis(out, 0, 1).reshape(b, S, *out.shape[3:])


def causal_conv(xcat, w, bias):
    L = xcat.shape[1] - (CONV_WIDTH - 1)
    out = bias
    for k in range(CONV_WIDTH):
        out = out + w[k] * xcat[:, k:k + L]
    return out


def ssd(x, dt, a, b_in, c_in, h0):
    bsz, L = x.shape[:2]
    chunk = min(SSD_CHUNK, L)
    n_chunks = -(-L // chunk)
    pad = n_chunks * chunk - L

    def prep(t):
        t = t.astype(F32)
        t = jnp.pad(t, [(0, 0), (0, pad)] + [(0, 0)] * (t.ndim - 2))
        return t.reshape(bsz, n_chunks, chunk, *t.shape[2:])

    x, dt, b_in, c_in = prep(x), prep(dt), prep(b_in), prep(c_in)
    cs = jnp.cumsum(dt * a, axis=2)
    xdt = x * dt[..., None]
    causal = jnp.tril(jnp.ones((chunk, chunk), bool))
    seg = cs[:, :, :, None, :] - cs[:, :, None, :, :]
    decay_in = jnp.exp(jnp.where(causal[:, :, None], seg, -jnp.inf))
    scores = jnp.einsum('bclhn,bcshn->bclsh', c_in, b_in) * decay_in
    y = jnp.einsum('bclsh,bcshp->bclhp', scores, xdt)
    decay_end = jnp.exp(cs[:, :, -1:, :] - cs)
    chunk_states = jnp.einsum('bclhn,bclhp->bchpn', b_in * decay_end[..., None], xdt)
    chunk_decay = jnp.exp(cs[:, :, -1, :])

    def step(h, inp):
        dec, st = inp
        return dec[:, :, None, None] * h + st, h

    h_last, h_start = lax.scan(step, h0.astype(F32),
                               (jnp.moveaxis(chunk_decay, 1, 0), jnp.moveaxis(chunk_states, 1, 0)))
    h_start = jnp.moveaxis(h_start, 0, 1)
    y = y + jnp.einsum('bclhn,bchpn->bclhp', c_in * jnp.exp(cs)[..., None], h_start)
    return y.reshape(bsz, n_chunks * chunk, *y.shape[3:])[:, :L], h_last


def token_mixers(u, pos, p, past):
    b, L, _ = u.shape
    proj = u @ p['w_in']
    splits = np.cumsum(IN_SIZES)[:-1].tolist()
    q_sb, k_sb, v_sb, q_a, c_kv, k_pe, z, xbc, dt = jnp.split(proj, splits, axis=-1)

    q_sb = q_sb.reshape(b, L, SB_HEADS, SB_HEAD_DIM)
    k_sb = k_sb.reshape(b, L, SB_HEADS, SB_HEAD_DIM)
    v_sb = v_sb.reshape(b, L, SB_HEADS, SB_HEAD_DIM)

    q = (rmsnorm(q_a, p['q_a_norm']) @ p['w_q_b']).reshape(b, L, MLA_HEADS, MLA_NOPE + MLA_ROPE)
    q_nope, q_pe = q[..., :MLA_NOPE], rope(q[..., MLA_NOPE:], pos)
    c_kv = rmsnorm(c_kv, p['kv_a_norm'])
    k_pe = rope(k_pe, pos)
    w_kv_b = p['w_kv_b'].reshape(MLA_KV_RANK, MLA_HEADS, MLA_NOPE + MLA_V)
    w_uk, w_uv = w_kv_b[..., :MLA_NOPE], w_kv_b[..., MLA_NOPE:]
    q_lat = jnp.einsum('blhn,rhn->blhr', q_nope, w_uk)

    conv_prev = jnp.zeros((b, CONV_WIDTH - 1, CONV_DIM), u.dtype) if past is None else past['conv']
    xcat = jnp.concatenate([conv_prev.astype(u.dtype), xbc], axis=1)
    conv_new = xcat[:, -(CONV_WIDTH - 1):]
    xbc_c = jax.nn.silu(causal_conv(xcat, p['conv_w'], p['conv_b']))
    xs, bm, cm = jnp.split(xbc_c, [SSM_INNER, SSM_INNER + SSM_GROUPS * SSM_STATE], axis=-1)
    xs = xs.reshape(b, L, SSM_HEADS, SSM_HEAD_DIM)
    rep = SSM_HEADS // SSM_GROUPS
    bm = jnp.repeat(bm.reshape(b, L, SSM_GROUPS, SSM_STATE), rep, axis=2)
    cm = jnp.repeat(cm.reshape(b, L, SSM_GROUPS, SSM_STATE), rep, axis=2)
    dt = jax.nn.softplus(dt.astype(F32) + p['dt_bias'].astype(F32))
    a = -jnp.exp(p['a_log'].astype(F32))
    h0 = jnp.zeros((b, SSM_HEADS, SSM_HEAD_DIM, SSM_STATE), F32) if past is None else past['ssm']
    y, h_new = ssd(xs, dt, a, bm, cm, h0)
    y = y + p['d_skip'].astype(F32)[:, None] * xs.astype(F32)
    y_c = gated_group_rmsnorm(y.reshape(b, L, SSM_INNER), z, p['ssm_norm'])

    if past is None:
        o_sb = query_blocks(lambda qb, pb: sb_attend(qb, pb, k_sb, v_sb, pos), (q_sb,), pos)
        o_lat = query_blocks(lambda ql, qp, pb: mla_attend(ql, qp, pb, c_kv, k_pe, pos), (q_lat, q_pe), pos)
    else:
        past_len = past['sb_k'].shape[1]
        k_pos = jnp.concatenate([jnp.arange(past_len, dtype=jnp.int32), pos])
        k_all = jnp.concatenate([past['sb_k'].astype(u.dtype), k_sb], axis=1)
        v_all = jnp.concatenate([past['sb_v'].astype(u.dtype), v_sb], axis=1)
        ckv_all = jnp.concatenate([past['ckv'].astype(u.dtype), c_kv], axis=1)
        kpe_all = jnp.concatenate([past['kpe'].astype(u.dtype), k_pe], axis=1)
        o_sb = sb_attend(q_sb, pos, k_all, v_all, k_pos)
        o_lat = mla_attend(q_lat, q_pe, pos, ckv_all, kpe_all, k_pos)
    o_mla = jnp.einsum('blhr,rhv->blhv', o_lat, w_uv).reshape(b, L, MLA_HEADS * MLA_V)

    mixed = jnp.concatenate([o_sb.reshape(b, L, SB_WIDTH), o_mla, y_c.astype(u.dtype)], axis=-1)
    out = mixed @ p['w_out']
    return out, (k_sb, v_sb, c_kv, k_pe, conv_new, h_new.astype(u.dtype))


def setup_inputs(seed: int = 0) -> dict:
    key = jax.random.key(seed)
    ks = jax.random.split(key, 40)
    n_pages = PAST_LEN // PAGE_SIZE
    n_used = DEC_BATCH * n_pages
    n_pool = n_used + n_used // 4

    def nrm(k, shape, scale=1.0):
        return jax.random.normal(k, shape, F32) * scale

    def gain(k, shape):
        return 1.0 + 0.01 * jax.random.normal(k, shape, F32)

    dt0 = jnp.exp(jax.random.uniform(ks[20], (DEPTH, SSM_HEADS), F32, np.log(1e-3), np.log(1e-1)))
    return {
        "x_prompt": nrm(ks[0], (BATCH, SEQ, D_MODEL)),
        "x_sample": nrm(ks[1], (DEC_BATCH, DEC_SEQ, D_MODEL)),
        "cache_sb_k": nrm(ks[2], (DEPTH, n_pool, PAGE_SIZE, SB_HEADS, SB_HEAD_DIM)),
        "cache_sb_v": nrm(ks[3], (DEPTH, n_pool, PAGE_SIZE, SB_HEADS, SB_HEAD_DIM)),
        "cache_mla_ckv": nrm(ks[4], (DEPTH, n_pool, PAGE_SIZE, MLA_KV_RANK)),
        "cache_mla_kpe": nrm(ks[5], (DEPTH, n_pool, PAGE_SIZE, MLA_ROPE)),
        "state_conv": nrm(ks[6], (DEPTH, DEC_BATCH, CONV_WIDTH - 1, CONV_DIM)),
        "state_ssm": nrm(ks[7], (DEPTH, DEC_BATCH, SSM_HEADS, SSM_HEAD_DIM, SSM_STATE), 0.5),
        "page_table": jax.random.permutation(ks[8], n_pool)[:n_used].reshape(DEC_BATCH, n_pages).astype(jnp.int32),
        "ln1": gain(ks[9], (DEPTH, D_MODEL)),
        "w_in": nrm(ks[10], (DEPTH, D_MODEL, IN_WIDTH), D_MODEL ** -0.5),
        "q_a_norm": gain(ks[11], (DEPTH, MLA_Q_RANK)),
        "w_q_b": nrm(ks[12], (DEPTH, MLA_Q_RANK, MLA_HEADS * (MLA_NOPE + MLA_ROPE)), MLA_Q_RANK ** -0.5),
        "kv_a_norm": gain(ks[13], (DEPTH, MLA_KV_RANK)),
        "w_kv_b": nrm(ks[14], (DEPTH, MLA_KV_RANK, MLA_HEADS * (MLA_NOPE + MLA_V)), MLA_KV_RANK ** -0.5),
        "conv_w": nrm(ks[15], (DEPTH, CONV_WIDTH, CONV_DIM), CONV_WIDTH ** -0.5),
        "conv_b": nrm(ks[16], (DEPTH, CONV_DIM), 0.01),
        "dt_bias": dt0 + jnp.log(-jnp.expm1(-dt0)),
        "a_log": jnp.log(jax.random.uniform(ks[17], (DEPTH, SSM_HEADS), F32, 1.0, 16.0)),
        "d_skip": gain(ks[18], (DEPTH, SSM_HEADS)),
        "ssm_norm": gain(ks[19], (DEPTH, SSM_INNER)),
        "w_out": nrm(ks[21], (DEPTH, MIX_WIDTH, D_MODEL), MIX_WIDTH ** -0.5),
        "ln2": gain(ks[22], (DEPTH, D_MODEL)),
        "w_gate": nrm(ks[23], (DEPTH, D_MODEL, FFN_HIDDEN), D_MODEL ** -0.5),
        "w_up": nrm(ks[24], (DEPTH, D_MODEL, FFN_HIDDEN), D_MODEL ** -0.5),
        "w_down": nrm(ks[25], (DEPTH, FFN_HIDDEN, D_MODEL), FFN_HIDDEN ** -0.5),
        "final_norm": gain(ks[26], (D_MODEL,)),
    }


def reference(x_prompt, x_sample, cache_sb_k, cache_sb_v, cache_mla_ckv, cache_mla_kpe, state_conv,
              state_ssm, page_table, ln1, w_in, q_a_norm, w_q_b, kv_a_norm, w_kv_b, conv_w, conv_b,
              dt_bias, a_log, d_skip, ssm_norm, w_out, ln2, w_gate, w_up, w_down, final_norm):
    layer_params = [dict(ln1=ln1[l], w_in=w_in[l], q_a_norm=q_a_norm[l], w_q_b=w_q_b[l],
                         kv_a_norm=kv_a_norm[l], w_kv_b=w_kv_b[l], conv_w=conv_w[l], conv_b=conv_b[l],
                         dt_bias=dt_bias[l], a_log=a_log[l], d_skip=d_skip[l], ssm_norm=ssm_norm[l],
                         w_out=w_out[l], ln2=ln2[l], w_gate=w_gate[l], w_up=w_up[l], w_down=w_down[l])
                    for l in range(DEPTH)]

    def trunk(x, pos, pasts):
        h = x
        states = []
        for l in range(DEPTH):
            p = layer_params[l]
            mix, st = token_mixers(rmsnorm(h, p['ln1']), pos, p, pasts[l])
            h = h + mix
            u = rmsnorm(h, p['ln2'])
            h = h + (jax.nn.silu(u @ p['w_gate']) * (u @ p['w_up'])) @ p['w_down']
            states.append(st)
        return rmsnorm(h, final_norm), [jnp.stack(s) for s in zip(*states)]

    def gather_pages(pool, l):
        g = pool[l, page_table]
        return g.reshape(g.shape[0], g.shape[1] * g.shape[2], *g.shape[3:])

    pos_p = jnp.arange(x_prompt.shape[1], dtype=jnp.int32)
    y_prompt, st_p = trunk(x_prompt, pos_p, [None] * DEPTH)
    p_sb_k, p_sb_v, p_ckv, p_kpe, p_conv, p_ssm = st_p

    past_len = page_table.shape[1] * PAGE_SIZE
    pos_s = past_len + jnp.arange(x_sample.shape[1], dtype=jnp.int32)
    pasts = [dict(sb_k=gather_pages(cache_sb_k, l), sb_v=gather_pages(cache_sb_v, l),
                  ckv=gather_pages(cache_mla_ckv, l), kpe=gather_pages(cache_mla_kpe, l),
                  conv=state_conv[l], ssm=state_ssm[l]) for l in range(DEPTH)]
    y_sample, st_s = trunk(x_sample, pos_s, pasts)
    s_sb_k, s_sb_v, s_ckv, s_kpe, s_conv, s_ssm = st_s

    return (y_prompt, y_sample, p_sb_k, p_sb_v, p_ckv, p_kpe, p_conv, p_ssm,
            s_sb_k, s_sb_v, s_ckv, s_kpe, s_conv, s_ssm)
```

```python
import functools
import math

import numpy as np
import jax
import jax.numpy as jnp
from jax import lax
from jax.experimental import pallas as pl
from jax.experimental.pallas import tpu as pltpu

F32 = jnp.float32
BF16 = jnp.bfloat16

MLA_HEADS = 4
MLA_NOPE = 64
MLA_V = 64
SSM_GROUPS = 2
CONV_WIDTH = 4
SSD_CHUNK = 128
ROPE_THETA = 10000.0
RMS_EPS = 1e-6

LANES = 128
VMEM_LIMIT = 56 * 1024 * 1024


def _cparams(*sem):
    return pltpu.CompilerParams(dimension_semantics=sem, vmem_limit_bytes=VMEM_LIMIT)


def _whole(shape):
    nd = len(shape)
    return pl.BlockSpec(shape, lambda *_: (0,) * nd, pipeline_mode=pl.Buffered(1))


def _dot(a, b):
    return jnp.dot(a, b, preferred_element_type=F32)


def _dot_nt(a, b):
    return lax.dot_general(a, b, (((1,), (1,)), ((), ())), preferred_element_type=F32)


def _dot_tn(a, b):
    return lax.dot_general(a, b, (((0,), (0,)), ((), ())), preferred_element_type=F32)


def _split2(x):
    hi = x.astype(BF16)
    lo = (x - hi.astype(F32)).astype(BF16)
    return hi, lo


def _split3(x):
    hi = x.astype(BF16)
    r = x - hi.astype(F32)
    mid = r.astype(BF16)
    lo = (r - mid.astype(F32)).astype(BF16)
    return hi, mid, lo


def _dot_f32_lhs(x, w):
    a, b, c = _split3(x)
    return _dot(a, w) + _dot(b, w) + _dot(c, w)


def _dot_f32_rhs(w, x):
    a, b, c = _split3(x)
    return _dot(w, a) + _dot(w, b) + _dot(w, c)


def _softplus(x):
    return jnp.maximum(x, 0.0) + jnp.log1p(jnp.exp(-jnp.abs(x)))


def _silu(x):
    return x / (1.0 + jnp.exp(-x))


def _rms(x, w):
    return x * lax.rsqrt(jnp.mean(x * x, axis=-1, keepdims=True) + RMS_EPS) * w


def _in_proj_kernel(h_ref, ln1_ref, w_ref, qan_ref, wq_ref, wuk_ref, kvn_ref, dtb_ref, freq_ref,
                    qsb_ref, k32_ref, k16_ref, v32_ref, v16_ref, qlat_ref, qpe_ref, ckv32_ref, kvcat_ref,
                    kpe32_ref, z_ref, xbc_ref, dt_ref, *, tm, seq_len, pos_base, segs, sb_scale, rope):
    u = _rms(h_ref[...], ln1_ref[...]).astype(BF16)

    def seg(name):
        a, b = segs[name]
        return _dot(u, w_ref[:, a:b])

    qsb_ref[...] = (seg("q_sb") * sb_scale).astype(BF16)
    k = seg("k_sb")
    k32_ref[...] = k
    k16_ref[...] = k.astype(BF16)
    v = seg("v_sb")
    v32_ref[...] = v
    v16_ref[...] = v.astype(BF16)

    row = lax.broadcasted_iota(jnp.int32, (tm, LANES), 0) + pl.program_id(0) * tm
    pos = (pos_base + lax.rem(row, seq_len)).astype(F32)
    ang = pos * freq_ref[...]
    lane = lax.broadcasted_iota(jnp.int32, (tm, LANES), 1)
    second_half = lax.rem(lane, rope) >= rope // 2
    cos_t = jnp.cos(ang)
    sin_t = jnp.sin(ang)
    sin_t = jnp.where(second_half, sin_t, -sin_t)

    qa = _rms(seg("q_a"), qan_ref[...]).astype(BF16)
    q = _dot(qa, wq_ref[...])
    nn = MLA_HEADS * MLA_NOPE
    qlat_ref[...] = _dot(q[:, :nn].astype(BF16), wuk_ref[...]).astype(BF16)
    qpe_ref[...] = (q[:, nn:nn + LANES] * cos_t + q[:, nn + LANES:] * sin_t).astype(BF16)

    ckv = _rms(seg("c_kv"), kvn_ref[...])
    ckv32_ref[...] = ckv
    kpe = seg("k_pe") * cos_t + seg("k_pe_sw") * sin_t
    kpe32_ref[...] = kpe[:, :rope]
    kv_rank = ckv.shape[1]
    kvcat_ref[:, :kv_rank] = ckv.astype(BF16)
    kvcat_ref[:, kv_rank:] = kpe.astype(BF16)

    z_ref[...] = seg("z")
    xbc_ref[...] = seg("xbc")
    dt_ref[...] = _softplus(seg("dt") + dtb_ref[...])


def _in_proj(h, lp, *, seq_len, pos_base, cfg):
    T, D = h.shape
    tm = min(512, T)
    assert T % tm == 0
    segs, NA = cfg["segs"], cfg["na"]
    sbw, kvr, rope = cfg["sb_width"], cfg["kv_rank"], cfg["rope"]
    qlw = MLA_HEADS * kvr
    kern = functools.partial(_in_proj_kernel, tm=tm, seq_len=seq_len, pos_base=pos_base, segs=segs,
                             sb_scale=cfg["sb_scale"], rope=rope)
    row = lambda w: pl.BlockSpec((tm, w), lambda i: (i, 0))
    outs = [
        (sbw, BF16), (sbw, F32), (sbw, BF16), (sbw, F32), (sbw, BF16), (qlw, BF16), (LANES, BF16),
        (kvr, F32), (kvr + LANES, BF16), (rope, F32), (cfg["ssm_inner"], F32), (cfg["conv_dim"], F32), (LANES, F32),
    ]
    return pl.pallas_call(
        kern,
        grid=(T // tm,),
        in_specs=[row(D), _whole((1, D)), _whole((D, NA)), _whole((1, cfg["q_rank"])), _whole(lp["wq"].shape),
                  _whole(lp["wuk"].shape), _whole((1, kvr)), _whole((1, LANES)), _whole((1, LANES))],
        out_specs=[row(w) for w, _ in outs],
        out_shape=[jax.ShapeDtypeStruct((T, w), dt) for w, dt in outs],
        compiler_params=_cparams("parallel"),
        name="in_proj",
    )(h, lp["ln1"], lp["wbig"], lp["qan"], lp["wq"], lp["wuk"], lp["kvn"], lp["dtb"], cfg["freq"])


def _sb_block(qm, kblk, vblk, tt, carry, valid, tk):
    z = _dot_nt(qm, kblk)
    lk = -_softplus(z)
    if valid is not None:
        lk = jnp.where(valid, lk, 0.0)
    hi, lo = _split2(lk)
    r = _dot(hi, tt) + _dot(lo, tt)
    e = z + r[:, :tk]
    if carry is not None:
        e = e + carry
    w = jnp.exp(e)
    if valid is not None:
        w = jnp.where(valid, w, 0.0)
    tot = r[:, tk:]
    return _dot(w.astype(BF16), vblk), (tot if carry is None else carry + tot)


def _sb_prompt_kernel(q_ref, k_ref, v_ref, tt_ref, o_ref, vm_ref, carry_ref, acc_ref, *, tq, heads, hd):
    i = pl.program_id(1)
    W = heads * hd
    lane_head = lax.broadcasted_iota(jnp.int32, (1, W), 1) // hd

    @pl.when(i == 0)
    def _():
        v = v_ref[...]
        for h in range(heads):
            vm_ref[h] = jnp.where(lane_head == h, v, jnp.zeros_like(v))

    q = q_ref[...]
    tt = tt_ref[...]
    qms = [jnp.where(lane_head == h, q, jnp.zeros_like(q)) for h in range(heads)]
    rows = lax.broadcasted_iota(jnp.int32, (tq, tq), 0)
    cols = lax.broadcasted_iota(jnp.int32, (tq, tq), 1)
    valid = cols < rows

    d0 = pl.multiple_of(i * tq, tq)
    kblk = k_ref[pl.ds(d0, tq), :]
    acc = jnp.zeros((tq, W), F32)
    for h in range(heads):
        pv, c = _sb_block(qms[h], kblk, vm_ref[h, pl.ds(d0, tq), :], tt, None, valid, tq)
        carry_ref[h] = c
        acc = acc + pv
    acc_ref[...] = acc

    def body(jj, _):
        s0 = pl.multiple_of((i - 1 - jj) * tq, tq)
        kb = k_ref[pl.ds(s0, tq), :]
        a = acc_ref[...]
        for h in range(heads):
            pv, c = _sb_block(qms[h], kb, vm_ref[h, pl.ds(s0, tq), :], tt, carry_ref[h], None, tq)
            carry_ref[h] = c
            a = a + pv
        acc_ref[...] = a
        return 0

    lax.fori_loop(0, i, body, 0)
    o_ref[...] = acc_ref[...].astype(BF16)


def _sb_prompt(q, k, v, *, B, S, cfg):
    T, W = q.shape
    tq = min(128, S)
    assert S % tq == 0
    heads, hd = cfg["sb_heads"], cfg["sb_hd"]
    nq = S // tq
    kern = functools.partial(_sb_prompt_kernel, tq=tq, heads=heads, hd=hd)
    return pl.pallas_call(
        kern,
        grid=(B, nq),
        in_specs=[pl.BlockSpec((tq, W), lambda b, i: (b * nq + i, 0)),
                  pl.BlockSpec((S, W), lambda b, i: (b, 0)),
                  pl.BlockSpec((S, W), lambda b, i: (b, 0)),
                  _whole((tq, 2 * tq))],
        out_specs=pl.BlockSpec((tq, W), lambda b, i: (b * nq + i, 0)),
        out_shape=jax.ShapeDtypeStruct((T, W), BF16),
        scratch_shapes=[pltpu.VMEM((heads, S, W), BF16), pltpu.VMEM((heads, tq, tq), F32),
                        pltpu.VMEM((tq, W), F32)],
        compiler_params=_cparams("parallel", "arbitrary"),
        name="sb_prompt",
    )(q, k, v, _rev_cumsum_matrix(tq))


def _rev_cumsum_matrix(tk):
    j = np.arange(tk)
    t = (j[:, None] >= j[None, :]).astype(np.float32)
    return jnp.asarray(np.concatenate([t, np.ones((tk, tk), np.float32)], axis=1), BF16)


def _mla_prompt_kernel(qlat_ref, qpe_ref, kv_ref, o_ref, qcat_ref, m_ref, l_ref, acc_ref, *, tq, kvr, rope, scale):
    i = pl.program_id(1)
    H = MLA_HEADS
    qpe = qpe_ref[...]
    pe_head = lax.broadcasted_iota(jnp.int32, (1, LANES), 1) // rope
    for h in range(H):
        qcat_ref[h * tq:(h + 1) * tq, :kvr] = qlat_ref[:, h * kvr:(h + 1) * kvr]
        qcat_ref[h * tq:(h + 1) * tq, kvr:] = jnp.where(pe_head == h, qpe, jnp.zeros_like(qpe))
    m_ref[...] = jnp.full(m_ref.shape, -jnp.inf, F32)
    l_ref[...] = jnp.zeros(l_ref.shape, F32)
    acc_ref[...] = jnp.zeros(acc_ref.shape, F32)

    def step(s0, masked):
        kv = kv_ref[pl.ds(s0, tq), :]
        s = _dot_nt(qcat_ref[...], kv) * scale
        if masked:
            qpos = lax.rem(lax.broadcasted_iota(jnp.int32, (H * tq, tq), 0), tq)
            kpos = lax.broadcasted_iota(jnp.int32, (H * tq, tq), 1)
            s = jnp.where(kpos <= qpos, s, -jnp.inf)
        m_old = m_ref[...]
        m_new = jnp.maximum(m_old, jnp.max(s, axis=-1, keepdims=True))
        alpha = jnp.exp(m_old - m_new)
        p = jnp.exp(s - m_new)
        l_ref[...] = alpha * l_ref[...] + jnp.sum(p, axis=-1, keepdims=True)
        acc_ref[...] = alpha * acc_ref[...] + _dot(p.astype(BF16), kv[:, :kvr])
        m_ref[...] = m_new

    def body(j, _):
        step(pl.multiple_of(j * tq, tq), False)
        return 0

    lax.fori_loop(0, i, body, 0)
    step(pl.multiple_of(i * tq, tq), True)
    o = acc_ref[...] / l_ref[...]
    for h in range(H):
        o_ref[:, h * kvr:(h + 1) * kvr] = o[h * tq:(h + 1) * tq].astype(BF16)


def _mla_prompt(qlat, qpe, kvcat, *, B, S, cfg):
    T = qlat.shape[0]
    tq = min(128, S)
    assert S % tq == 0
    nq = S // tq
    kvr, rope = cfg["kv_rank"], cfg["rope"]
    kern = functools.partial(_mla_prompt_kernel, tq=tq, kvr=kvr, rope=rope, scale=cfg["mla_scale"])
    H = MLA_HEADS
    return pl.pallas_call(
        kern,
        grid=(B, nq),
        in_specs=[pl.BlockSpec((tq, H * kvr), lambda b, i: (b * nq + i, 0)),
                  pl.BlockSpec((tq, LANES), lambda b, i: (b * nq + i, 0)),
                  pl.BlockSpec((S, kvr + LANES), lambda b, i: (b, 0))],
        out_specs=pl.BlockSpec((tq, H * kvr), lambda b, i: (b * nq + i, 0)),
        out_shape=jax.ShapeDtypeStruct((T, H * kvr), BF16),
        scratch_shapes=[pltpu.VMEM((H * tq, kvr + LANES), BF16), pltpu.VMEM((H * tq, 1), F32),
                        pltpu.VMEM((H * tq, 1), F32), pltpu.VMEM((H * tq, kvr), F32)],
        compiler_params=_cparams("parallel", "arbitrary"),
        name="mla_prompt",
    )(qlat, qpe, kvcat)


def _gated_norm(y, z, nw, groups):
    g = y * _silu(z)
    gw = g.shape[-1] // groups
    outs = []
    for k in range(groups):
        gg = g[:, k * gw:(k + 1) * gw]
        outs.append(gg * lax.rsqrt(jnp.mean(gg * gg, axis=-1, keepdims=True) + RMS_EPS))
    return jnp.concatenate(outs, axis=-1) * nw


def _ssd_prompt_kernel(xbc_ref, dt_ref, z_ref, cw_ref, cb_ref, alog_ref, dsk_ref, nw_ref, e_ref, tl_ref,
                       y_ref, conv_ref, hlast_ref, xpad_ref, hs_ref, *, L, heads, hd, n_state, inner):
    c = pl.program_id(1)
    nc = pl.num_programs(1)
    PADR = 8
    gh = heads // SSM_GROUPS
    gw = gh * hd

    @pl.when(c == 0)
    def _():
        xpad_ref[0:PADR, :] = jnp.zeros((PADR, xpad_ref.shape[1]), F32)
        hs_ref[...] = jnp.zeros(hs_ref.shape, F32)

    x_in = xbc_ref[...]
    xpad_ref[PADR:PADR + L, :] = x_in
    conv = cb_ref[...]
    for k in range(CONV_WIDTH - 1):
        off = PADR - (CONV_WIDTH - 1) + k
        conv = conv + cw_ref[k:k + 1, :] * xpad_ref[off:off + L, :]
    conv = conv + cw_ref[CONV_WIDTH - 1:CONV_WIDTH, :] * x_in
    xc = _silu(conv)
    xs = xc[:, :inner]
    bm = xc[:, inner:inner + SSM_GROUPS * n_state]
    cm = xc[:, inner + SSM_GROUPS * n_state:]

    lane = lax.broadcasted_iota(jnp.int32, (1, LANES), 1)
    dt = jnp.where(lane < heads, dt_ref[...], 0.0)
    a = -jnp.exp(alog_ref[...])
    da = dt * a
    tl = tl_ref[...]
    e = e_ref[...]
    cs = _dot_f32_rhs(tl, da)
    cs_t = cs.T
    tot = cs[L - 1:L, :]
    ecs_exp = _dot_f32_lhs(jnp.exp(cs), e)
    dend_exp = _dot_f32_lhs(jnp.exp(tot - cs), e)
    dt_exp = _dot_f32_lhs(dt, e)
    xdt = xs * dt_exp
    xdt16 = xdt.astype(BF16)
    xd16 = (xdt * dend_exp).astype(BF16)

    hs = hs_ref[...]
    hs16 = hs.astype(BF16)
    rows = lax.broadcasted_iota(jnp.int32, (L, L), 0)
    cols = lax.broadcasted_iota(jnp.int32, (L, L), 1)
    tril = cols <= rows
    pair_lane = lax.broadcasted_iota(jnp.int32, (1, 2 * hd), 1) // hd
    etot = jnp.exp(tot)

    y_parts = []
    for g in range(SSM_GROUPS):
        cg = cm[:, g * n_state:(g + 1) * n_state].astype(BF16)
        bg = bm[:, g * n_state:(g + 1) * n_state].astype(BF16)
        scores = _dot_nt(cg, bg)
        y_off = _dot_nt(cg, hs16[g * gw:(g + 1) * gw, :]) * ecs_exp[:, g * gw:(g + 1) * gw]
        pairs = []
        for hp in range(gh // 2):
            h0 = g * gh + 2 * hp
            xpair = xdt16[:, h0 * hd:(h0 + 2) * hd]
            yp = None
            for t in range(2):
                h = h0 + t
                seg = cs[:, h:h + 1] - cs_t[h:h + 1, :]
                dec = jnp.exp(jnp.where(tril, seg, -jnp.inf))
                m = (scores * dec).astype(BF16)
                xm = jnp.where(pair_lane == t, xpair, jnp.zeros_like(xpair))
                d = _dot(m, xm)
                yp = d if yp is None else yp + d
            pairs.append(yp)
        y_parts.append(jnp.concatenate(pairs, axis=-1) + y_off)
        st = _dot_tn(xd16[:, g * gw:(g + 1) * gw], bg)
        for k in range(gh):
            h = g * gh + k
            r0 = h * hd
            hs_ref[r0:r0 + hd, :] = etot[:, h:h + 1] * hs[r0:r0 + hd, :] + st[k * hd:(k + 1) * hd, :]

    y = jnp.concatenate(y_parts, axis=-1) + dsk_ref[...] * xs
    y_ref[...] = _gated_norm(y, z_ref[...], nw_ref[...], SSM_GROUPS).astype(BF16)
    xpad_ref[0:PADR, :] = xpad_ref[L:L + PADR, :]

    @pl.when(c == nc - 1)
    def _():
        conv_ref[0] = xpad_ref[L + PADR - (CONV_WIDTH - 1):L + PADR, :]
        for h in range(heads):
            hlast_ref[0, h] = hs_ref[h * hd:(h + 1) * hd, :]


def _ssd_prompt(xbc, dt, z, lp, *, B, S, cfg):
    T = xbc.shape[0]
    L = min(SSD_CHUNK, S)
    assert S % L == 0 and L >= 8
    nc = S // L
    heads, hd, n_state, inner, cd = cfg["ssm_heads"], cfg["ssm_hd"], cfg["ssm_state"], cfg["ssm_inner"], cfg["conv_dim"]
    kern = functools.partial(_ssd_prompt_kernel, L=L, heads=heads, hd=hd, n_state=n_state, inner=inner)
    row = lambda w: pl.BlockSpec((L, w), lambda b, c: (b * nc + c, 0))
    tl = jnp.asarray(np.tril(np.ones((L, L), np.float32)), BF16)
    return pl.pallas_call(
        kern,
        grid=(B, nc),
        in_specs=[row(cd), row(LANES), row(inner), _whole((CONV_WIDTH, cd)), _whole((1, cd)), _whole((1, LANES)),
                  _whole((1, inner)), _whole((1, inner)), _whole((LANES, inner)), _whole((L, L))],
        out_specs=[row(inner),
                   pl.BlockSpec((1, CONV_WIDTH - 1, cd), lambda b, c: (b, 0, 0)),
                   pl.BlockSpec((1, heads, hd, n_state), lambda b, c: (b, 0, 0, 0))],
        out_shape=[jax.ShapeDtypeStruct((T, inner), BF16),
                   jax.ShapeDtypeStruct((B, CONV_WIDTH - 1, cd), F32),
                   jax.ShapeDtypeStruct((B, heads, hd, n_state), F32)],
        scratch_shapes=[pltpu.VMEM((L + 8, cd), F32), pltpu.VMEM((heads * hd, n_state), F32)],
        compiler_params=_cparams("parallel", "arbitrary"),
        name="ssd_prompt",
    )(xbc, dt, z, lp["conv_w"], lp["conv_b"], lp["alog"], lp["dsk"], lp["nw"], cfg["expand"], tl)


def _out_ffn_kernel(h_ref, osb_ref, olat_ref, yc_ref, wuv_ref, wo_ref, ln2_ref, wg_ref, wu_ref, wd_ref, fn_ref,
                    o_ref, *, sbw, mlaw, hidden, hc, final):
    omla = _dot(olat_ref[...], wuv_ref[...]).astype(BF16)
    mix = (_dot(osb_ref[...], wo_ref[0:sbw, :]) + _dot(omla, wo_ref[sbw:sbw + mlaw, :])
           + _dot(yc_ref[...], wo_ref[sbw + mlaw:, :]))
    h1 = h_ref[...] + mix
    u = _rms(h1, ln2_ref[...]).astype(BF16)
    acc = h1
    for c0 in range(0, hidden, hc):
        g = _dot(u, wg_ref[:, c0:c0 + hc])
        up = _dot(u, wu_ref[:, c0:c0 + hc])
        acc = acc + _dot((_silu(g) * up).astype(BF16), wd_ref[c0:c0 + hc, :])
    o_ref[...] = _rms(acc, fn_ref[...]) if final else acc


def _out_ffn(h, osb, olat, yc, lp, fn, *, final, cfg):
    T, D = h.shape
    tm = min(512, T)
    assert T % tm == 0
    hidden = lp["wg"].shape[1]
    hc = 256 if hidden % 256 == 0 else hidden
    sbw = osb.shape[1]
    mlaw = lp["wuv"].shape[1]
    kern = functools.partial(_out_ffn_kernel, sbw=sbw, mlaw=mlaw, hidden=hidden, hc=hc, final=final)
    row = lambda w: pl.BlockSpec((tm, w), lambda i: (i, 0))
    return pl.pallas_call(
        kern,
        grid=(T // tm,),
        in_specs=[row(D), row(sbw), row(olat.shape[1]), row(yc.shape[1]), _whole(lp["wuv"].shape),
                  _whole(lp["wo"].shape), _whole((1, D)), _whole(lp["wg"].shape), _whole(lp["wu"].shape),
                  _whole(lp["wd"].shape), _whole((1, D))],
        out_specs=row(D),
        out_shape=jax.ShapeDtypeStruct((T, D), F32),
        compiler_params=_cparams("parallel"),
        name="out_ffn",
    )(h, osb, olat, yc, lp["wuv"], lp["wo"], lp["ln2"], lp["wg"], lp["wu"], lp["wd"], fn)


def _dec_attn_kernel(pt_ref, qsb_ref, qlat_ref, qpe_ref, kvn_ref, tt_ref, *refs, P, page, sbw, sb_hd, kvr, rope,
                     scale):
    del pt_ref
    k_refs, v_refs, c_refs, p_refs = refs[0:P], refs[P:2 * P], refs[2 * P:3 * P], refs[3 * P:4 * P]
    osb_ref, olat_ref = refs[4 * P], refs[4 * P + 1]
    qm_ref, ql_ref, qp_ref, carry_ref, accs_ref, m_ref, l_ref, accl_ref = refs[4 * P + 2:]
    c = pl.program_id(1)
    nc = pl.num_programs(1)
    R = 8
    H = MLA_HEADS

    @pl.when(c == 0)
    def _():
        row_s = lax.broadcasted_iota(jnp.int32, (R, sbw), 0)
        head_s = lax.broadcasted_iota(jnp.int32, (R, sbw), 1) // sb_hd
        q = jnp.broadcast_to(qsb_ref[...].astype(F32), (R, sbw))
        qm_ref[...] = jnp.where(row_s == head_s, q, 0.0).astype(BF16)
        row_l = lax.broadcasted_iota(jnp.int32, (R, kvr), 0)
        row_p = lax.broadcasted_iota(jnp.int32, (R, rope), 0)
        ql = jnp.zeros((R, kvr), F32)
        qp = jnp.zeros((R, rope), F32)
        qlat = qlat_ref[...].astype(F32)
        qpe = qpe_ref[...].astype(F32)
        for h in range(H):
            ql = jnp.where(row_l == h, jnp.broadcast_to(qlat[:, h * kvr:(h + 1) * kvr], (R, kvr)), ql)
            qp = jnp.where(row_p == h, jnp.broadcast_to(qpe[:, h * rope:(h + 1) * rope], (R, rope)), qp)
        ql_ref[...] = ql.astype(BF16)
        qp_ref[...] = qp.astype(BF16)
        kvn = kvn_ref[...].astype(F32)
        ckv_new = kvn[:, :kvr]
        kpe_new = kvn[:, kvr:kvr + rope]
        s_self = (jnp.sum(ql * ckv_new, axis=-1, keepdims=True)
                  + jnp.sum(qp * kpe_new, axis=-1, keepdims=True)) * scale
        m_ref[...] = s_self
        l_ref[...] = jnp.ones((R, 1), F32)
        accl_ref[...] = jnp.broadcast_to(ckv_new, (R, kvr))
        carry_ref[...] = jnp.zeros((R, page), F32)
        accs_ref[...] = jnp.zeros((R, sbw), F32)

    tt = tt_ref[...]
    qm = qm_ref[...]
    ql = ql_ref[...]
    qp = qp_ref[...]
    carry = carry_ref[...]
    accs = accs_ref[...]
    m = m_ref[...]
    l = l_ref[...]
    accl = accl_ref[...]
    for p in range(P):
        kp = k_refs[p][...].astype(BF16)
        vp = v_refs[p][...].astype(BF16)
        cp = c_refs[p][...].astype(BF16)
        pp = p_refs[p][...].astype(BF16)
        z = _dot_nt(qm, kp)
        hi, lo = _split2(-_softplus(z))
        r = _dot(hi, tt) + _dot(lo, tt)
        w = jnp.exp(z + r[:, :page] + carry)
        carry = carry + r[:, page:]
        accs = accs + _dot(w.astype(BF16), vp)
        s = (_dot_nt(ql, cp) + _dot_nt(qp, pp)) * scale
        m_new = jnp.maximum(m, jnp.max(s, axis=-1, keepdims=True))
        alpha = jnp.exp(m - m_new)
        pr = jnp.exp(s - m_new)
        l = alpha * l + jnp.sum(pr, axis=-1, keepdims=True)
        accl = alpha * accl + _dot(pr.astype(BF16), cp)
        m = m_new
    carry_ref[...] = carry
    accs_ref[...] = accs
    m_ref[...] = m
    l_ref[...] = l
    accl_ref[...] = accl

    @pl.when(c == nc - 1)
    def _():
        row_s = lax.broadcasted_iota(jnp.int32, (R, sbw), 0)
        head_s = lax.broadcasted_iota(jnp.int32, (R, sbw), 1) // sb_hd
        osb_ref[...] = jnp.sum(jnp.where(row_s == head_s, accs, 0.0), axis=0, keepdims=True).astype(BF16)
        o = accl / l
        for h in range(H):
            olat_ref[:, h * kvr:(h + 1) * kvr] = o[h:h + 1, :].astype(BF16)


def _dec_attn(qsb, qlat, qpe, kvcat, caches, page_table, layer, *, cfg):
    k4, v4, c4, p4 = caches
    Bd, NP = page_table.shape
    page = k4.shape[2]
    sbw, kvr, rope = cfg["sb_width"], cfg["kv_rank"], cfg["rope"]
    P = 8 if NP % 8 == 0 else (2 if NP % 2 == 0 else 1)
    nc = NP // P

    def page_spec(width, p):
        return pl.BlockSpec((None, None, page, width),
                            lambda b, c, pt, p=p: (layer, pt[b, NP - 1 - c * P - p], 0, 0))

    vec = lambda w: pl.BlockSpec((None, 1, w), lambda b, c, pt: (b, 0, 0))
    in_specs = [vec(sbw), vec(MLA_HEADS * kvr), vec(LANES), vec(kvr + LANES),
                pl.BlockSpec((page, 2 * page), lambda b, c, pt: (0, 0))]
    args = [qsb[:, None, :], qlat[:, None, :], qpe[:, None, :], kvcat[:, None, :], _rev_cumsum_matrix(page)]
    for arr, width in ((k4, sbw), (v4, sbw), (c4, kvr), (p4, rope)):
        for p in range(P):
            in_specs.append(page_spec(width, p))
            args.append(arr)
    kern = functools.partial(_dec_attn_kernel, P=P, page=page, sbw=sbw, sb_hd=cfg["sb_hd"], kvr=kvr, rope=rope,
                             scale=cfg["mla_scale"])
    R = 8
    osb, olat = pl.pallas_call(
        kern,
        grid_spec=pltpu.PrefetchScalarGridSpec(
            num_scalar_prefetch=1,
            grid=(Bd, nc),
            in_specs=in_specs,
            out_specs=[vec(sbw), vec(MLA_HEADS * kvr)],
            scratch_shapes=[pltpu.VMEM((R, sbw), BF16), pltpu.VMEM((R, kvr), BF16), pltpu.VMEM((R, rope), BF16),
                            pltpu.VMEM((R, page), F32), pltpu.VMEM((R, sbw), F32), pltpu.VMEM((R, 1), F32),
                            pltpu.VMEM((R, 1), F32), pltpu.VMEM((R, kvr), F32)],
        ),
        out_shape=[jax.ShapeDtypeStruct((Bd, 1, sbw), BF16), jax.ShapeDtypeStruct((Bd, 1, MLA_HEADS * kvr), BF16)],
        compiler_params=_cparams("parallel", "arbitrary"),
        name="dec_attn",
    )(page_table, *args)
    return osb[:, 0, :], olat[:, 0, :]


def _ssd_decode_kernel(xbc_ref, cst_ref, dt_ref, z_ref, h_ref, cw_ref, cb_ref, alog_ref, dsk_ref, nw_ref, e_ref,
                       y_ref, cnew_ref, hnew_ref, *, heads, hd, n_state, inner):
    R = 8
    gh = heads // SSM_GROUPS
    gw = gh * hd
    xr = xbc_ref[...]
    st = cst_ref[...]
    conv = cb_ref[...]
    for k in range(CONV_WIDTH - 1):
        conv = conv + cw_ref[k:k + 1, :] * st[k:k + 1, :]
    conv = conv + cw_ref[CONV_WIDTH - 1:CONV_WIDTH, :] * xr
    for k in range(CONV_WIDTH - 2):
        cnew_ref[k:k + 1, :] = st[k + 1:k + 2, :]
    cnew_ref[CONV_WIDTH - 2:CONV_WIDTH - 1, :] = xr
    xc = _silu(conv)
    xs = xc[:, :inner]
    bm = xc[:, inner:inner + SSM_GROUPS * n_state]
    cm = xc[:, inner + SSM_GROUPS * n_state:]

    lane = lax.broadcasted_iota(jnp.int32, (1, LANES), 1)
    dt = jnp.where(lane < heads, dt_ref[...], 0.0)
    da = dt * (-jnp.exp(alog_ref[...]))
    e = e_ref[...]
    row8 = lambda x: jnp.broadcast_to(x, (R, x.shape[1]))
    dt_exp = _dot_f32_lhs(row8(dt), e)[0:1, :]
    dec_exp = _dot_f32_lhs(row8(jnp.exp(da)), e)[0:1, :]
    xdt = xs * dt_exp

    rid = lambda w: lax.broadcasted_iota(jnp.int32, (R, w), 0)
    first = lambda x: jnp.where(rid(x.shape[1]) == 0, row8(x), 0.0).astype(BF16)
    d_hi = dec_exp.astype(BF16).astype(F32)
    d_mid = (dec_exp - d_hi).astype(BF16).astype(F32)
    d_lo = dec_exp - d_hi - d_mid
    dec_rows = jnp.zeros((R, inner), F32)
    for k, piece in enumerate((d_hi, d_mid, d_lo)):
        dec_rows = jnp.where(rid(inner) == k, row8(piece), dec_rows)
    dec_rows = dec_rows.astype(BF16)
    ones_rows = jnp.where(rid(n_state) < 3, 1.0, 0.0).astype(BF16)
    x_rows = first(xdt)

    y_parts = []
    for g in range(SSM_GROUPS):
        cg = cm[:, g * n_state:(g + 1) * n_state]
        bg = bm[:, g * n_state:(g + 1) * n_state]
        c_rows = first(cg)
        b_rows = first(bg)
        h0 = jnp.concatenate([h_ref[g * gh + k] for k in range(gh)], axis=0)
        y_off = _dot_nt(c_rows, h0.astype(BF16))[0:1, :]
        cb = jnp.sum(cg * bg, axis=-1, keepdims=True)
        sl = slice(g * gw, (g + 1) * gw)
        y_parts.append(cb * xdt[:, sl] + y_off * dec_exp[:, sl])
        dec_tile = _dot_tn(dec_rows[:, sl], ones_rows)
        outer = _dot_tn(x_rows[:, sl], b_rows)
        hn = dec_tile * h0 + outer
        for k in range(gh):
            hnew_ref[g * gh + k] = hn[k * hd:(k + 1) * hd, :]
    y = jnp.concatenate(y_parts, axis=-1) + dsk_ref[...] * xs
    y_ref[...] = _gated_norm(y, z_ref[...], nw_ref[...], SSM_GROUPS).astype(BF16)


def _ssd_decode(xbc, dt, z, state_conv, state_ssm, layer, lp, *, cfg):
    Bd = xbc.shape[0]
    heads, hd, n_state, inner, cd = cfg["ssm_heads"], cfg["ssm_hd"], cfg["ssm_state"], cfg["ssm_inner"], cfg["conv_dim"]
    kern = functools.partial(_ssd_decode_kernel, heads=heads, hd=hd, n_state=n_state, inner=inner)
    vec = lambda w: pl.BlockSpec((None, 1, w), lambda b: (b, 0, 0))
    y, cnew, hnew = pl.pallas_call(
        kern,
        grid=(Bd,),
        in_specs=[vec(cd),
                  pl.BlockSpec((None, None, CONV_WIDTH - 1, cd), lambda b: (layer, b, 0, 0)),
                  vec(LANES), vec(inner),
                  pl.BlockSpec((None, None, heads, hd, n_state), lambda b: (layer, b, 0, 0, 0)),
                  _whole((CONV_WIDTH, cd)), _whole((1, cd)), _whole((1, LANES)), _whole((1, inner)),
                  _whole((1, inner)), _whole((LANES, inner))],
        out_specs=[vec(inner),
                   pl.BlockSpec((None, CONV_WIDTH - 1, cd), lambda b: (b, 0, 0)),
                   pl.BlockSpec((None, heads, hd, n_state), lambda b: (b, 0, 0, 0))],
        out_shape=[jax.ShapeDtypeStruct((Bd, 1, inner), BF16),
                   jax.ShapeDtypeStruct((Bd, CONV_WIDTH - 1, cd), F32),
                   jax.ShapeDtypeStruct((Bd, heads, hd, n_state), F32)],
        compiler_params=_cparams("parallel"),
        name="ssd_decode",
    )(xbc[:, None, :], state_conv, dt[:, None, :], z[:, None, :], state_ssm, lp["conv_w"], lp["conv_b"], lp["alog"],
      lp["dsk"], lp["nw"], cfg["expand"])
    return y[:, 0, :], cnew, hnew


def _layer_params(l, cfg, ln1, w_in, q_a_norm, w_q_b, kv_a_norm, w_kv_b, conv_w, conv_b, dt_bias, a_log, d_skip,
                  ssm_norm, w_out, ln2, w_gate, w_up, w_down):
    sbw, qr, kvr, rope = cfg["sb_width"], cfg["q_rank"], cfg["kv_rank"], cfg["rope"]
    inner, cd, heads, hd = cfg["ssm_inner"], cfg["conv_dim"], cfg["ssm_heads"], cfg["ssm_hd"]
    w = w_in[l]
    o = 3 * sbw + qr + kvr
    kpe_w = w[:, o:o + rope]
    half = rope // 2
    kpe_sw = jnp.concatenate([kpe_w[:, half:], kpe_w[:, :half]], axis=1)
    rep = LANES // rope
    o2 = o + rope
    dt_w = w[:, o2 + inner + cd:]
    wbig = jnp.concatenate([w[:, :o], jnp.tile(kpe_w, (1, rep)), jnp.tile(kpe_sw, (1, rep)),
                            w[:, o2:o2 + inner + cd], jnp.pad(dt_w, ((0, 0), (0, LANES - heads)))], axis=1)
    H = MLA_HEADS
    wq3 = w_q_b[l].reshape(qr, H, MLA_NOPE + rope)
    pe = wq3[:, :, MLA_NOPE:]
    pe_sw = jnp.concatenate([pe[:, :, half:], pe[:, :, :half]], axis=2)
    wq = jnp.concatenate([wq3[:, :, :MLA_NOPE].reshape(qr, H * MLA_NOPE), pe.reshape(qr, H * rope),
                          pe_sw.reshape(qr, H * rope)], axis=1)
    wkv3 = w_kv_b[l].reshape(kvr, H, MLA_NOPE + MLA_V)
    eye = jnp.eye(H, dtype=F32)
    wuk = jnp.einsum("rhn,hg->hngr", wkv3[:, :, :MLA_NOPE], eye).reshape(H * MLA_NOPE, H * kvr)
    wuv = jnp.einsum("rhv,hg->hrgv", wkv3[:, :, MLA_NOPE:], eye).reshape(H * kvr, H * MLA_V)
    pad_l = lambda x: jnp.pad(x, (0, LANES - x.shape[0]))[None, :]
    return dict(
        ln1=ln1[l][None, :], wbig=wbig.astype(BF16), qan=q_a_norm[l][None, :], wq=wq.astype(BF16),
        wuk=wuk.astype(BF16), wuv=wuv.astype(BF16), kvn=kv_a_norm[l][None, :], dtb=pad_l(dt_bias[l]),
        conv_w=conv_w[l], conv_b=conv_b[l][None, :], alog=pad_l(a_log[l]), dsk=jnp.repeat(d_skip[l], hd)[None, :],
        nw=ssm_norm[l][None, :], wo=w_out[l].astype(BF16), ln2=ln2[l][None, :], wg=w_gate[l].astype(BF16),
        wu=w_up[l].astype(BF16), wd=w_down[l].astype(BF16),
    )


def _config(cache_sb_k, cache_mla_ckv, cache_mla_kpe, state_ssm, q_a_norm):
    sb_heads, sb_hd = cache_sb_k.shape[3], cache_sb_k.shape[4]
    kvr, rope = cache_mla_ckv.shape[3], cache_mla_kpe.shape[3]
    heads, hd, n_state = state_ssm.shape[2], state_ssm.shape[3], state_ssm.shape[4]
    sbw, qr, inner = sb_heads * sb_hd, q_a_norm.shape[1], heads * hd
    cd = inner + 2 * SSM_GROUPS * n_state
    assert LANES % rope == 0 and LANES // rope >= MLA_HEADS and heads <= LANES and heads % (2 * SSM_GROUPS) == 0
    assert 2 * hd == LANES and n_state == LANES and sbw % LANES == 0 and kvr % LANES == 0 and qr % LANES == 0
    segs, o = {}, 0
    for name, wd in (("q_sb", sbw), ("k_sb", sbw), ("v_sb", sbw), ("q_a", qr), ("c_kv", kvr), ("k_pe", LANES),
                     ("k_pe_sw", LANES), ("z", inner), ("xbc", cd), ("dt", LANES)):
        segs[name] = (o, o + wd)
        o += wd
    half = rope // 2
    freq = np.float32(ROPE_THETA) ** (-np.arange(half, dtype=np.float32) / np.float32(half))
    freq = np.tile(freq.astype(np.float32), LANES // half)[None, :]
    expand = np.zeros((LANES, inner), np.float32)
    for h in range(heads):
        expand[h, h * hd:(h + 1) * hd] = 1.0
    return dict(sb_heads=sb_heads, sb_hd=sb_hd, sb_width=sbw, sb_scale=float(sb_hd) ** -0.5, q_rank=qr, kv_rank=kvr,
                rope=rope, mla_scale=float(MLA_NOPE + rope) ** -0.5, ssm_heads=heads, ssm_hd=hd, ssm_state=n_state,
                ssm_inner=inner, conv_dim=cd, segs=segs, na=o, freq=jnp.asarray(freq),
                expand=jnp.asarray(expand, BF16))


def kernel(x_prompt, x_sample, cache_sb_k, cache_sb_v, cache_mla_ckv, cache_mla_kpe, state_conv, state_ssm, page_table, ln1, w_in, q_a_norm, w_q_b, kv_a_norm, w_kv_b, conv_w, conv_b, dt_bias, a_log, d_skip, ssm_norm, w_out, ln2, w_gate, w_up, w_down, final_norm):
    B, S, D = x_prompt.shape
    Bd, Sd, _ = x_sample.shape
    assert Sd == 1, "the decode kernels handle one new token per sequence"
    depth = w_in.shape[0]
    cfg = _config(cache_sb_k, cache_mla_ckv, cache_mla_kpe, state_ssm, q_a_norm)
    assert math.log2(cfg["sb_scale"]).is_integer()
    n_pool, page = cache_sb_k.shape[1], cache_sb_k.shape[2]
    past_len = page_table.shape[1] * page
    caches = (cache_sb_k.reshape(depth, n_pool, page, cfg["sb_width"]),
              cache_sb_v.reshape(depth, n_pool, page, cfg["sb_width"]), cache_mla_ckv, cache_mla_kpe)
    fn = final_norm[None, :]
    lps = [_layer_params(l, cfg, ln1, w_in, q_a_norm, w_q_b, kv_a_norm, w_kv_b, conv_w, conv_b, dt_bias, a_log, d_skip,
                         ssm_norm, w_out, ln2, w_gate, w_up, w_down) for l in range(depth)]

    h = x_prompt.reshape(B * S, D)
    p_state = []
    for l, lp in enumerate(lps):
        (qsb, k32, k16, v32, v16, qlat, qpe, ckv32, kvcat, kpe32, z, xbc, dt) = _in_proj(
            h, lp, seq_len=S, pos_base=0, cfg=cfg)
        osb = _sb_prompt(qsb, k16, v16, B=B, S=S, cfg=cfg)
        olat = _mla_prompt(qlat, qpe, kvcat, B=B, S=S, cfg=cfg)
        yc, conv_new, h_last = _ssd_prompt(xbc, dt, z, lp, B=B, S=S, cfg=cfg)
        h = _out_ffn(h, osb, olat, yc, lp, fn, final=(l == depth - 1), cfg=cfg)
        p_state.append((k32.reshape(B, S, cfg["sb_heads"], cfg["sb_hd"]), v32.reshape(B, S, cfg["sb_heads"], cfg["sb_hd"]),
                        ckv32.reshape(B, S, -1), kpe32.reshape(B, S, -1), conv_new, h_last))
    y_prompt = h.reshape(B, S, D)

    h = x_sample.reshape(Bd, D)
    s_state = []
    for l, lp in enumerate(lps):
        (qsb, k32, k16, v32, v16, qlat, qpe, ckv32, kvcat, kpe32, z, xbc, dt) = _in_proj(
            h, lp, seq_len=Sd, pos_base=past_len, cfg=cfg)
        osb, olat = _dec_attn(qsb, qlat, qpe, kvcat, caches, page_table, l, cfg=cfg)
        yc, conv_new, h_new = _ssd_decode(xbc, dt, z, state_conv, state_ssm, l, lp, cfg=cfg)
        h = _out_ffn(h, osb, olat, yc, lp, fn, final=(l == depth - 1), cfg=cfg)
        s_state.append((k32.reshape(Bd, Sd, cfg["sb_heads"], cfg["sb_hd"]), v32.reshape(Bd, Sd, cfg["sb_heads"], cfg["sb_hd"]),
                        ckv32.reshape(Bd, Sd, -1), kpe32.reshape(Bd, Sd, -1), conv_new, h_new))
    y_sample = h.reshape(Bd, Sd, D)

    stack = lambda states: tuple(jnp.stack(s) for s in zip(*states))
    return (y_prompt, y_sample) + stack(p_state) + stack(s_state)
```

```python
import functools
import math

import numpy as np
import jax
import jax.numpy as jnp
from jax import lax
from jax.experimental import pallas as pl
from jax.experimental.pallas import tpu as pltpu

F32 = jnp.float32
BF16 = jnp.bfloat16

MLA_HEADS = 4
MLA_NOPE = 64
MLA_V = 64
SSM_GROUPS = 2
CONV_WIDTH = 4
SSD_CHUNK = 128
ROPE_THETA = 10000.0
RMS_EPS = 1e-6

LANES = 128
VMEM_LIMIT = 56 * 1024 * 1024


def _cparams(*sem):
    return pltpu.CompilerParams(dimension_semantics=sem, vmem_limit_bytes=VMEM_LIMIT)


def _whole(shape):
    nd = len(shape)
    return pl.BlockSpec(shape, lambda *_: (0,) * nd, pipeline_mode=pl.Buffered(1))


def _dot(a, b):
    return jnp.dot(a, b, preferred_element_type=F32)


def _dot_nt(a, b):
    return lax.dot_general(a, b, (((1,), (1,)), ((), ())), preferred_element_type=F32)


def _dot_tn(a, b):
    return lax.dot_general(a, b, (((0,), (0,)), ((), ())), preferred_element_type=F32)


def _split2(x):
    hi = x.astype(BF16)
    lo = (x - hi.astype(F32)).astype(BF16)
    return hi, lo


def _split3(x):
    hi = x.astype(BF16)
    r = x - hi.astype(F32)
    mid = r.astype(BF16)
    lo = (r - mid.astype(F32)).astype(BF16)
    return hi, mid, lo


def _dot_f32_lhs(x, w):
    a, b, c = _split3(x)
    return _dot(a, w) + _dot(b, w) + _dot(c, w)


def _dot_f32_rhs(w, x):
    a, b, c = _split3(x)
    return _dot(w, a) + _dot(w, b) + _dot(w, c)


def _softplus(x):
    return jnp.maximum(x, 0.0) + jnp.log1p(jnp.exp(-jnp.abs(x)))


def _silu(x):
    return x / (1.0 + jnp.exp(-x))


def _rms(x, w):
    return x * lax.rsqrt(jnp.mean(x * x, axis=-1, keepdims=True) + RMS_EPS) * w


def _in_proj_kernel(h_ref, ln1_ref, w_ref, qan_ref, wq_ref, wuk_ref, kvn_ref, dtb_ref, freq_ref,
                    qsb_ref, k32_ref, k16_ref, v32_ref, v16_ref, qlat_ref, qpe_ref, ckv32_ref, kvcat_ref,
                    kpe32_ref, z_ref, xbc_ref, dt_ref, *, tm, seq_len, pos_base, segs, sb_scale, rope):
    u = _rms(h_ref[...], ln1_ref[...]).astype(BF16)

    def seg(name):
        a, b = segs[name]
        return _dot(u, w_ref[:, a:b])

    qsb_ref[...] = (seg("q_sb") * sb_scale).astype(BF16)
    k = seg("k_sb")
    k32_ref[...] = k
    k16_ref[...] = k.astype(BF16)
    v = seg("v_sb")
    v32_ref[...] = v
    v16_ref[...] = v.astype(BF16)

    row = lax.broadcasted_iota(jnp.int32, (tm, LANES), 0) + pl.program_id(0) * tm
    pos = (pos_base + lax.rem(row, seq_len)).astype(F32)
    ang = pos * freq_ref[...]
    lane = lax.broadcasted_iota(jnp.int32, (tm, LANES), 1)
    second_half = lax.rem(lane, rope) >= rope // 2
    cos_t = jnp.cos(ang)
    sin_t = jnp.sin(ang)
    sin_t = jnp.where(second_half, sin_t, -sin_t)

    qa = _rms(seg("q_a"), qan_ref[...]).astype(BF16)
    q = _dot(qa, wq_ref[...])
    nn = MLA_HEADS * MLA_NOPE
    qlat_ref[...] = _dot(q[:, :nn].astype(BF16), wuk_ref[...]).astype(BF16)
    qpe_ref[...] = (q[:, nn:nn + LANES] * cos_t + q[:, nn + LANES:] * sin_t).astype(BF16)

    ckv = _rms(seg("c_kv"), kvn_ref[...])
    ckv32_ref[...] = ckv
    kpe = seg("k_pe") * cos_t + seg("k_pe_sw") * sin_t
    kpe32_ref[...] = kpe[:, :rope]
    kv_rank = ckv.shape[1]
    kvcat_ref[:, :kv_rank] = ckv.astype(BF16)
    kvcat_ref[:, kv_rank:] = kpe.astype(BF16)

    z_ref[...] = seg("z")
    xbc_ref[...] = seg("xbc")
    dt_ref[...] = _softplus(seg("dt") + dtb_ref[...])


def _in_proj(h, lp, *, seq_len, pos_base, cfg):
    T, D = h.shape
    tm = min(512, T)
    assert T % tm == 0
    segs, NA = cfg["segs"], cfg["na"]
    sbw, kvr, rope = cfg["sb_width"], cfg["kv_rank"], cfg["rope"]
    qlw = MLA_HEADS * kvr
    kern = functools.partial(_in_proj_kernel, tm=tm, seq_len=seq_len, pos_base=pos_base, segs=segs,
                             sb_scale=cfg["sb_scale"], rope=rope)
    row = lambda w: pl.BlockSpec((tm, w), lambda i: (i, 0))
    outs = [
        (sbw, BF16), (sbw, F32), (sbw, BF16), (sbw, F32), (sbw, BF16), (qlw, BF16), (LANES, BF16),
        (kvr, F32), (kvr + LANES, BF16), (rope, F32), (cfg["ssm_inner"], F32), (cfg["conv_dim"], F32), (LANES, F32),
    ]
    return pl.pallas_call(
        kern,
        grid=(T // tm,),
        in_specs=[row(D), _whole((1, D)), _whole((D, NA)), _whole((1, cfg["q_rank"])), _whole(lp["wq"].shape),
                  _whole(lp["wuk"].shape), _whole((1, kvr)), _whole((1, LANES)), _whole((1, LANES))],
        out_specs=[row(w) for w, _ in outs],
        out_shape=[jax.ShapeDtypeStruct((T, w), dt) for w, dt in outs],
        compiler_params=_cparams("parallel"),
        name="in_proj",
    )(h, lp["ln1"], lp["wbig"], lp["qan"], lp["wq"], lp["wuk"], lp["kvn"], lp["dtb"], cfg["freq"])


def _sb_step(qms, k_ref, vm_ref, tt2, carry_ref, acc_ref, s0, r0, valid, tk):
    kb = k_ref[pl.ds(s0, tk), :]
    a = acc_ref[r0:, :]
    for h, qm in enumerate(qms):
        z = _dot_nt(qm[r0:, :], kb)
        lk = -_softplus(z)
        if valid is not None:
            lk = jnp.where(valid, lk, 0.0)
        hi, lo = _split2(lk)
        r = _dot(jnp.concatenate([hi, lo], axis=1), tt2)
        c = carry_ref[h, r0:, :]
        w = jnp.exp(z + r[:, :tk] + c)
        if valid is not None:
            w = jnp.where(valid, w, 0.0)
        carry_ref[h, r0:, :] = c + r[:, tk:]
        a = a + _dot(w.astype(BF16), vm_ref[h, pl.ds(s0, tk), :])
    acc_ref[r0:, :] = a


def _sb_prompt_kernel(q_ref, k_ref, v_ref, tt_ref, o_ref, vm_ref, carry_ref, acc_ref, *, tq, tk, heads, hd, unroll):
    i = pl.program_id(1)
    W = heads * hd
    nb = tq // tk
    lane_head = lax.broadcasted_iota(jnp.int32, (1, W), 1) // hd

    @pl.when(i == 0)
    def _():
        v = v_ref[...]
        for h in range(heads):
            vm_ref[h] = jnp.where(lane_head == h, v, jnp.zeros_like(v))

    q = q_ref[...]
    tt2 = tt_ref[...]
    qms = [jnp.where(lane_head == h, q, jnp.zeros_like(q)) for h in range(heads)]
    acc_ref[...] = jnp.zeros(acc_ref.shape, F32)
    carry_ref[...] = jnp.zeros(carry_ref.shape, F32)

    for d in reversed(range(nb)):
        r0 = d * tk
        rows = lax.broadcasted_iota(jnp.int32, (tq - r0, tk), 0)
        cols = lax.broadcasted_iota(jnp.int32, (tq - r0, tk), 1)
        _sb_step(qms, k_ref, vm_ref, tt2, carry_ref, acc_ref, pl.multiple_of(i * tq + r0, tk), r0, cols < rows, tk)

    def body(jj, _):
        s0 = pl.multiple_of((i * nb - 1 - jj) * tk, tk)
        _sb_step(qms, k_ref, vm_ref, tt2, carry_ref, acc_ref, s0, 0, None, tk)
        return 0

    lax.fori_loop(0, i * nb, body, 0, unroll=unroll)
    o_ref[...] = acc_ref[...].astype(BF16)


SB_TQ, SB_TK, SB_UNROLL = 512, 128, 1


def _sb_prompt(q, k, v, *, B, S, cfg):
    T, W = q.shape
    tk = min(SB_TK, S)
    tq = min(SB_TQ, S)
    assert S % tq == 0 and tq % tk == 0
    heads, hd = cfg["sb_heads"], cfg["sb_hd"]
    nq = S // tq
    kern = functools.partial(_sb_prompt_kernel, tq=tq, tk=tk, heads=heads, hd=hd, unroll=SB_UNROLL)
    return pl.pallas_call(
        kern,
        grid=(B, nq),
        in_specs=[pl.BlockSpec((tq, W), lambda b, i: (b * nq + i, 0)),
                  pl.BlockSpec((S, W), lambda b, i: (b, 0)),
                  pl.BlockSpec((S, W), lambda b, i: (b, 0)),
                  _whole((2 * tk, 2 * tk))],
        out_specs=pl.BlockSpec((tq, W), lambda b, i: (b * nq + i, 0)),
        out_shape=jax.ShapeDtypeStruct((T, W), BF16),
        scratch_shapes=[pltpu.VMEM((heads, S, W), BF16), pltpu.VMEM((heads, tq, tk), F32),
                        pltpu.VMEM((tq, W), F32)],
        compiler_params=_cparams("parallel", "arbitrary"),
        name="sb_prompt",
    )(q, k, v, _rev_cumsum_matrix(tk, stacked=True))


def _rev_cumsum_matrix(tk, stacked=False):
    j = np.arange(tk)
    t = (j[:, None] >= j[None, :]).astype(np.float32)
    m = np.concatenate([t, np.ones((tk, tk), np.float32)], axis=1)
    return jnp.asarray(np.concatenate([m, m], axis=0) if stacked else m, BF16)


def _mla_prompt_kernel(qlat_ref, qpe_ref, kv_ref, o_ref, qcat_ref, vx_ref, m_ref, acc_ref, *, tq, kvr, rope, scale):
    i = pl.program_id(1)
    H = MLA_HEADS
    M = H * tq
    nl = tq // LANES
    na = acc_ref.shape[1] // LANES

    @pl.when(i == 0)
    def _():
        vx_ref[:, :kvr] = kv_ref[:, :kvr]
        vx_ref[:, kvr:] = jnp.ones((vx_ref.shape[0], LANES), BF16)

    qpe = qpe_ref[...]
    pe_head = lax.broadcasted_iota(jnp.int32, (1, LANES), 1) // rope
    for h in range(H):
        qcat_ref[h * tq:(h + 1) * tq, :kvr] = qlat_ref[:, h * kvr:(h + 1) * kvr]
        qcat_ref[h * tq:(h + 1) * tq, kvr:] = jnp.where(pe_head == h, qpe, jnp.zeros_like(qpe))
    m_ref[...] = jnp.full(m_ref.shape, -jnp.inf, F32)
    acc_ref[...] = jnp.zeros(acc_ref.shape, F32)

    def step(s0, masked):
        s = _dot_nt(qcat_ref[...], kv_ref[pl.ds(s0, tq), :]) * scale
        if masked:
            qpos = lax.rem(lax.broadcasted_iota(jnp.int32, (M, tq), 0), tq)
            kpos = lax.broadcasted_iota(jnp.int32, (M, tq), 1)
            s = jnp.where(kpos <= qpos, s, -jnp.inf)
        chunks = [s[:, c * LANES:(c + 1) * LANES] for c in range(nl)]
        cmax = functools.reduce(jnp.maximum, chunks)
        m_old = m_ref[...]
        m_new = jnp.maximum(m_old, jnp.max(cmax, axis=-1, keepdims=True))
        alpha = jnp.exp(m_old - m_new)
        p = jnp.concatenate([jnp.exp(c - m_new) for c in chunks], axis=1).astype(BF16)
        pv = _dot(p, vx_ref[pl.ds(s0, tq), :])
        for c in range(na):
            sl = slice(c * LANES, (c + 1) * LANES)
            acc_ref[:, sl] = alpha * acc_ref[:, sl] + pv[:, sl]
        m_ref[...] = m_new

    def body(j, _):
        step(pl.multiple_of(j * tq, tq), False)
        return 0

    lax.fori_loop(0, i, body, 0)
    step(pl.multiple_of(i * tq, tq), True)
    inv = 1.0 / acc_ref[:, kvr:kvr + LANES]
    for h in range(H):
        for c in range(kvr // LANES):
            o_ref[:, h * kvr + c * LANES:h * kvr + (c + 1) * LANES] = (
                acc_ref[h * tq:(h + 1) * tq, c * LANES:(c + 1) * LANES] * inv[h * tq:(h + 1) * tq]).astype(BF16)


MLA_TQ = 512


def _mla_prompt(qlat, qpe, kvcat, *, B, S, cfg):
    T = qlat.shape[0]
    tq = min(MLA_TQ, S)
    assert S % tq == 0 and tq % LANES == 0
    nq = S // tq
    kvr, rope = cfg["kv_rank"], cfg["rope"]
    kern = functools.partial(_mla_prompt_kernel, tq=tq, kvr=kvr, rope=rope, scale=cfg["mla_scale"])
    H = MLA_HEADS
    return pl.pallas_call(
        kern,
        grid=(B, nq),
        in_specs=[pl.BlockSpec((tq, H * kvr), lambda b, i: (b * nq + i, 0)),
                  pl.BlockSpec((tq, LANES), lambda b, i: (b * nq + i, 0)),
                  pl.BlockSpec((S, kvr + LANES), lambda b, i: (b, 0))],
        out_specs=pl.BlockSpec((tq, H * kvr), lambda b, i: (b * nq + i, 0)),
        out_shape=jax.ShapeDtypeStruct((T, H * kvr), BF16),
        scratch_shapes=[pltpu.VMEM((H * tq, kvr + LANES), BF16), pltpu.VMEM((S, kvr + LANES), BF16),
                        pltpu.VMEM((H * tq, LANES), F32), pltpu.VMEM((H * tq, kvr + LANES), F32)],
        compiler_params=_cparams("parallel", "arbitrary"),
        name="mla_prompt",
    )(qlat, qpe, kvcat)


def _gated_norm(y, z, nw, groups):
    g = y * _silu(z)
    gw = g.shape[-1] // groups
    outs = []
    for k in range(groups):
        gg = g[:, k * gw:(k + 1) * gw]
        outs.append(gg * lax.rsqrt(jnp.mean(gg * gg, axis=-1, keepdims=True) + RMS_EPS))
    return jnp.concatenate(outs, axis=-1) * nw


def _ssd_prompt_kernel(xbc_ref, dt_ref, z_ref, cw_ref, cb_ref, alog_ref, dsk_ref, nw_ref, e_ref, tl_ref,
                       y_ref, conv_ref, hlast_ref, xpad_ref, hs_ref, *, L, heads, hd, n_state, inner):
    c = pl.program_id(1)
    nc = pl.num_programs(1)
    PADR = 8
    gh = heads // SSM_GROUPS
    gw = gh * hd

    @pl.when(c == 0)
    def _():
        xpad_ref[0:PADR, :] = jnp.zeros((PADR, xpad_ref.shape[1]), F32)
        hs_ref[...] = jnp.zeros(hs_ref.shape, F32)

    x_in = xbc_ref[...]
    xpad_ref[PADR:PADR + L, :] = x_in
    conv = cb_ref[...]
    for k in range(CONV_WIDTH - 1):
        off = PADR - (CONV_WIDTH - 1) + k
        conv = conv + cw_ref[k:k + 1, :] * xpad_ref[off:off + L, :]
    conv = conv + cw_ref[CONV_WIDTH - 1:CONV_WIDTH, :] * x_in
    xc = _silu(conv)
    xs = xc[:, :inner]
    bm = xc[:, inner:inner + SSM_GROUPS * n_state]
    cm = xc[:, inner + SSM_GROUPS * n_state:]

    lane = lax.broadcasted_iota(jnp.int32, (1, LANES), 1)
    dt = jnp.where(lane < heads, dt_ref[...], 0.0)
    a = -jnp.exp(alog_ref[...])
    da = dt * a
    tl = tl_ref[...]
    e = e_ref[...]
    cs = _dot_f32_rhs(tl, da)
    cs_t = cs.T
    tot = cs[L - 1:L, :]
    ecs_exp = _dot_f32_lhs(jnp.exp(cs), e)
    dend_exp = _dot_f32_lhs(jnp.exp(tot - cs), e)
    dt_exp = _dot_f32_lhs(dt, e)
    xdt = xs * dt_exp
    xdt16 = xdt.astype(BF16)
    xd16 = (xdt * dend_exp).astype(BF16)

    hs = hs_ref[...]
    hs16 = hs.astype(BF16)
    rows = lax.broadcasted_iota(jnp.int32, (L, L), 0)
    cols = lax.broadcasted_iota(jnp.int32, (L, L), 1)
    tril = cols <= rows
    pair_lane = lax.broadcasted_iota(jnp.int32, (1, 2 * hd), 1) // hd
    etot = jnp.exp(tot)

    y_parts = []
    for g in range(SSM_GROUPS):
        cg = cm[:, g * n_state:(g + 1) * n_state].astype(BF16)
        bg = bm[:, g * n_state:(g + 1) * n_state].astype(BF16)
        scores = _dot_nt(cg, bg)
        y_off = _dot_nt(cg, hs16[g * gw:(g + 1) * gw, :]) * ecs_exp[:, g * gw:(g + 1) * gw]
        pairs = []
        for hp in range(gh // 2):
            h0 = g * gh + 2 * hp
            xpair = xdt16[:, h0 * hd:(h0 + 2) * hd]
            yp = None
            for t in range(2):
                h = h0 + t
                seg = cs[:, h:h + 1] - cs_t[h:h + 1, :]
                dec = jnp.exp(jnp.where(tril, seg, -jnp.inf))
                m = (scores * dec).astype(BF16)
                xm = jnp.where(pair_lane == t, xpair, jnp.zeros_like(xpair))
                d = _dot(m, xm)
                yp = d if yp is None else yp + d
            pairs.append(yp)
        y_parts.append(jnp.concatenate(pairs, axis=-1) + y_off)
        st = _dot_tn(xd16[:, g * gw:(g + 1) * gw], bg)
        for k in range(gh):
            h = g * gh + k
            r0 = h * hd
            hs_ref[r0:r0 + hd, :] = etot[:, h:h + 1] * hs[r0:r0 + hd, :] + st[k * hd:(k + 1) * hd, :]

    y = jnp.concatenate(y_parts, axis=-1) + dsk_ref[...] * xs
    y_ref[...] = _gated_norm(y, z_ref[...], nw_ref[...], SSM_GROUPS).astype(BF16)
    xpad_ref[0:PADR, :] = xpad_ref[L:L + PADR, :]

    @pl.when(c == nc - 1)
    def _():
        conv_ref[0] = xpad_ref[L + PADR - (CONV_WIDTH - 1):L + PADR, :]
        for h in range(heads):
            hlast_ref[0, h] = hs_ref[h * hd:(h + 1) * hd, :]


def _ssd_prompt(xbc, dt, z, lp, *, B, S, cfg):
    T = xbc.shape[0]
    L = min(SSD_CHUNK, S)
    assert S % L == 0 and L >= 8
    nc = S // L
    heads, hd, n_state, inner, cd = cfg["ssm_heads"], cfg["ssm_hd"], cfg["ssm_state"], cfg["ssm_inner"], cfg["conv_dim"]
    kern = functools.partial(_ssd_prompt_kernel, L=L, heads=heads, hd=hd, n_state=n_state, inner=inner)
    row = lambda w: pl.BlockSpec((L, w), lambda b, c: (b * nc + c, 0))
    tl = jnp.asarray(np.tril(np.ones((L, L), np.float32)), BF16)
    return pl.pallas_call(
        kern,
        grid=(B, nc),
        in_specs=[row(cd), row(LANES), row(inner), _whole((CONV_WIDTH, cd)), _whole((1, cd)), _whole((1, LANES)),
                  _whole((1, inner)), _whole((1, inner)), _whole((LANES, inner)), _whole((L, L))],
        out_specs=[row(inner),
                   pl.BlockSpec((1, CONV_WIDTH - 1, cd), lambda b, c: (b, 0, 0)),
                   pl.BlockSpec((1, heads, hd, n_state), lambda b, c: (b, 0, 0, 0))],
        out_shape=[jax.ShapeDtypeStruct((T, inner), BF16),
                   jax.ShapeDtypeStruct((B, CONV_WIDTH - 1, cd), F32),
                   jax.ShapeDtypeStruct((B, heads, hd, n_state), F32)],
        scratch_shapes=[pltpu.VMEM((L + 8, cd), F32), pltpu.VMEM((heads * hd, n_state), F32)],
        compiler_params=_cparams("parallel", "arbitrary"),
        name="ssd_prompt",
    )(xbc, dt, z, lp["conv_w"], lp["conv_b"], lp["alog"], lp["dsk"], lp["nw"], cfg["expand"], tl)


def _out_ffn_kernel(h_ref, osb_ref, olat_ref, yc_ref, wuv_ref, wo_ref, ln2_ref, wg_ref, wu_ref, wd_ref, fn_ref,
                    o_ref, *, sbw, mlaw, hidden, hc, final):
    omla = _dot(olat_ref[...], wuv_ref[...]).astype(BF16)
    mix = (_dot(osb_ref[...], wo_ref[0:sbw, :]) + _dot(omla, wo_ref[sbw:sbw + mlaw, :])
           + _dot(yc_ref[...], wo_ref[sbw + mlaw:, :]))
    h1 = h_ref[...] + mix
    u = _rms(h1, ln2_ref[...]).astype(BF16)
    acc = h1
    for c0 in range(0, hidden, hc):
        g = _dot(u, wg_ref[:, c0:c0 + hc])
        up = _dot(u, wu_ref[:, c0:c0 + hc])
        acc = acc + _dot((_silu(g) * up).astype(BF16), wd_ref[c0:c0 + hc, :])
    o_ref[...] = _rms(acc, fn_ref[...]) if final else acc


def _out_ffn(h, osb, olat, yc, lp, fn, *, final, cfg):
    T, D = h.shape
    tm = min(512, T)
    assert T % tm == 0
    hidden = lp["wg"].shape[1]
    hc = 256 if hidden % 256 == 0 else hidden
    sbw = osb.shape[1]
    mlaw = lp["wuv"].shape[1]
    kern = functools.partial(_out_ffn_kernel, sbw=sbw, mlaw=mlaw, hidden=hidden, hc=hc, final=final)
    row = lambda w: pl.BlockSpec((tm, w), lambda i: (i, 0))
    return pl.pallas_call(
        kern,
        grid=(T // tm,),
        in_specs=[row(D), row(sbw), row(olat.shape[1]), row(yc.shape[1]), _whole(lp["wuv"].shape),
                  _whole(lp["wo"].shape), _whole((1, D)), _whole(lp["wg"].shape), _whole(lp["wu"].shape),
                  _whole(lp["wd"].shape), _whole((1, D))],
        out_specs=row(D),
        out_shape=jax.ShapeDtypeStruct((T, D), F32),
        compiler_params=_cparams("parallel"),
        name="out_ffn",
    )(h, osb, olat, yc, lp["wuv"], lp["wo"], lp["ln2"], lp["wg"], lp["wu"], lp["wd"], fn)


def _dec_attn_kernel(pt_ref, qsb_ref, qlat_ref, qpe_ref, kvn_ref, tt_ref, *refs, P, page, sbw, sb_hd, kvr, rope,
                     scale):
    del pt_ref
    k_refs, v_refs, c_refs, p_refs = refs[0:P], refs[P:2 * P], refs[2 * P:3 * P], refs[3 * P:4 * P]
    osb_ref, olat_ref = refs[4 * P], refs[4 * P + 1]
    qm_ref, ql_ref, qp_ref, carry_ref, accs_ref, m_ref, l_ref, accl_ref = refs[4 * P + 2:]
    c = pl.program_id(1)
    nc = pl.num_programs(1)
    R = 8
    H = MLA_HEADS

    @pl.when(c == 0)
    def _():
        row_s = lax.broadcasted_iota(jnp.int32, (R, sbw), 0)
        head_s = lax.broadcasted_iota(jnp.int32, (R, sbw), 1) // sb_hd
        q = jnp.broadcast_to(qsb_ref[...].astype(F32), (R, sbw))
        qm_ref[...] = jnp.where(row_s == head_s, q, 0.0).astype(BF16)
        row_l = lax.broadcasted_iota(jnp.int32, (R, kvr), 0)
        row_p = lax.broadcasted_iota(jnp.int32, (R, rope), 0)
        ql = jnp.zeros((R, kvr), F32)
        qp = jnp.zeros((R, rope), F32)
        qlat = qlat_ref[...].astype(F32)
        qpe = qpe_ref[...].astype(F32)
        for h in range(H):
            ql = jnp.where(row_l == h, jnp.broadcast_to(qlat[:, h * kvr:(h + 1) * kvr], (R, kvr)), ql)
            qp = jnp.where(row_p == h, jnp.broadcast_to(qpe[:, h * rope:(h + 1) * rope], (R, rope)), qp)
        ql_ref[...] = ql.astype(BF16)
        qp_ref[...] = qp.astype(BF16)
        kvn = kvn_ref[...].astype(F32)
        ckv_new = kvn[:, :kvr]
        kpe_new = kvn[:, kvr:kvr + rope]
        s_self = (jnp.sum(ql * ckv_new, axis=-1, keepdims=True)
                  + jnp.sum(qp * kpe_new, axis=-1, keepdims=True)) * scale
        m_ref[...] = s_self
        l_ref[...] = jnp.ones((R, 1), F32)
        accl_ref[...] = jnp.broadcast_to(ckv_new, (R, kvr))
        carry_ref[...] = jnp.zeros((R, page), F32)
        accs_ref[...] = jnp.zeros((R, sbw), F32)

    qm = qm_ref[...]
    ql = ql_ref[...]
    qp = qp_ref[...]
    zs = [_dot(qm, k_refs[p][...].astype(BF16)) for p in range(P)]
    cps = [c_refs[p][...].astype(BF16) for p in range(P)]
    ss = [(_dot_nt(ql, cps[p]) + _dot(qp, p_refs[p][...].astype(BF16))) * scale for p in range(P)]

    hi, lo = _split2(jnp.concatenate([-_softplus(z) for z in zs], axis=0))
    r = _dot(jnp.concatenate([hi, lo], axis=1), tt_ref[...])
    carry = carry_ref[...]
    accs = accs_ref[...]
    for p in range(P):
        rp = r[p * R:(p + 1) * R, :]
        w = jnp.exp(zs[p] + rp[:, :page] + carry)
        carry = carry + rp[:, page:]
        accs = accs + _dot_nt(w.astype(BF16), v_refs[p][...].astype(BF16))
    carry_ref[...] = carry
    accs_ref[...] = accs

    m = m_ref[...]
    m_new = jnp.maximum(m, jnp.max(functools.reduce(jnp.maximum, ss), axis=-1, keepdims=True))
    alpha = jnp.exp(m - m_new)
    prs = [jnp.exp(s - m_new) for s in ss]
    l_ref[...] = alpha * l_ref[...] + jnp.sum(functools.reduce(jnp.add, prs), axis=-1, keepdims=True)
    accl = alpha * accl_ref[...]
    for p in range(P):
        accl = accl + _dot(prs[p].astype(BF16), cps[p])
    accl_ref[...] = accl
    m_ref[...] = m_new

    @pl.when(c == nc - 1)
    def _():
        row_s = lax.broadcasted_iota(jnp.int32, (R, sbw), 0)
        head_s = lax.broadcasted_iota(jnp.int32, (R, sbw), 1) // sb_hd
        osb_ref[...] = jnp.sum(jnp.where(row_s == head_s, accs_ref[...], 0.0), axis=0, keepdims=True).astype(BF16)
        o = accl_ref[...] / l_ref[...]
        for h in range(H):
            olat_ref[:, h * kvr:(h + 1) * kvr] = o[h:h + 1, :].astype(BF16)


DEC_PAGES_PER_STEP = 16


def _dec_attn(qsb, qlat, qpe, kvcat, caches, page_table, layer, *, cfg):
    kt4, vt4, c4, pt4 = caches
    Bd, NP = page_table.shape
    page = c4.shape[2]
    sbw, kvr, rope = cfg["sb_width"], cfg["kv_rank"], cfg["rope"]
    P = next(p for p in (DEC_PAGES_PER_STEP, 8, 4, 2, 1) if NP % p == 0)
    nc = NP // P

    def page_spec(rows, cols, p):
        return pl.BlockSpec((None, None, rows, cols),
                            lambda b, c, pt, p=p: (layer, pt[b, NP - 1 - c * P - p], 0, 0))

    vec = lambda w: pl.BlockSpec((None, 1, w), lambda b, c, pt: (b, 0, 0))
    in_specs = [vec(sbw), vec(MLA_HEADS * kvr), vec(LANES), vec(kvr + LANES),
                pl.BlockSpec((2 * page, 2 * page), lambda b, c, pt: (0, 0))]
    args = [qsb[:, None, :], qlat[:, None, :], qpe[:, None, :], kvcat[:, None, :],
            _rev_cumsum_matrix(page, stacked=True)]
    for arr, rows, cols in ((kt4, sbw, page), (vt4, sbw, page), (c4, page, kvr), (pt4, rope, page)):
        for p in range(P):
            in_specs.append(page_spec(rows, cols, p))
            args.append(arr)
    kern = functools.partial(_dec_attn_kernel, P=P, page=page, sbw=sbw, sb_hd=cfg["sb_hd"], kvr=kvr, rope=rope,
                             scale=cfg["mla_scale"])
    R = 8
    osb, olat = pl.pallas_call(
        kern,
        grid_spec=pltpu.PrefetchScalarGridSpec(
            num_scalar_prefetch=1,
            grid=(Bd, nc),
            in_specs=in_specs,
            out_specs=[vec(sbw), vec(MLA_HEADS * kvr)],
            scratch_shapes=[pltpu.VMEM((R, sbw), BF16), pltpu.VMEM((R, kvr), BF16), pltpu.VMEM((R, rope), BF16),
                            pltpu.VMEM((R, page), F32), pltpu.VMEM((R, sbw), F32), pltpu.VMEM((R, 1), F32),
                            pltpu.VMEM((R, 1), F32), pltpu.VMEM((R, kvr), F32)],
        ),
        out_shape=[jax.ShapeDtypeStruct((Bd, 1, sbw), BF16), jax.ShapeDtypeStruct((Bd, 1, MLA_HEADS * kvr), BF16)],
        compiler_params=_cparams("parallel", "arbitrary"),
        name="dec_attn",
    )(page_table, *args)
    return osb[:, 0, :], olat[:, 0, :]


def _ssd_decode_kernel(xbc_ref, cst_ref, dt_ref, z_ref, h_ref, cw_ref, cb_ref, alog_ref, dsk_ref, nw_ref, e_ref,
                       y_ref, cnew_ref, hnew_ref, *, heads, hd, n_state, inner):
    R = 8
    gh = heads // SSM_GROUPS
    gw = gh * hd
    xr = xbc_ref[...]
    st = cst_ref[...]
    conv = cb_ref[...]
    for k in range(CONV_WIDTH - 1):
        conv = conv + cw_ref[k:k + 1, :] * st[k:k + 1, :]
    conv = conv + cw_ref[CONV_WIDTH - 1:CONV_WIDTH, :] * xr
    for k in range(CONV_WIDTH - 2):
        cnew_ref[k:k + 1, :] = st[k + 1:k + 2, :]
    cnew_ref[CONV_WIDTH - 2:CONV_WIDTH - 1, :] = xr
    xc = _silu(conv)
    xs = xc[:, :inner]
    bm = xc[:, inner:inner + SSM_GROUPS * n_state]
    cm = xc[:, inner + SSM_GROUPS * n_state:]

    lane = lax.broadcasted_iota(jnp.int32, (1, LANES), 1)
    dt = jnp.where(lane < heads, dt_ref[...], 0.0)
    da = dt * (-jnp.exp(alog_ref[...]))
    e = e_ref[...]
    row8 = lambda x: jnp.broadcast_to(x, (R, x.shape[1]))
    dt_exp = _dot_f32_lhs(row8(dt), e)[0:1, :]
    dec_exp = _dot_f32_lhs(row8(jnp.exp(da)), e)[0:1, :]
    xdt = xs * dt_exp

    rid = lambda w: lax.broadcasted_iota(jnp.int32, (R, w), 0)
    first = lambda x: jnp.where(rid(x.shape[1]) == 0, row8(x), 0.0).astype(BF16)
    d_hi = dec_exp.astype(BF16).astype(F32)
    d_mid = (dec_exp - d_hi).astype(BF16).astype(F32)
    d_lo = dec_exp - d_hi - d_mid
    dec_rows = jnp.zeros((R, inner), F32)
    for k, piece in enumerate((d_hi, d_mid, d_lo)):
        dec_rows = jnp.where(rid(inner) == k, row8(piece), dec_rows)
    dec_rows = dec_rows.astype(BF16)
    ones_rows = jnp.where(rid(n_state) < 3, 1.0, 0.0).astype(BF16)
    x_rows = first(xdt)

    y_parts = []
    for g in range(SSM_GROUPS):
        cg = cm[:, g * n_state:(g + 1) * n_state]
        bg = bm[:, g * n_state:(g + 1) * n_state]
        c_rows = first(cg)
        b_rows = first(bg)
        h0 = jnp.concatenate([h_ref[g * gh + k] for k in range(gh)], axis=0)
        y_off = _dot_nt(c_rows, h0.astype(BF16))[0:1, :]
        cb = jnp.sum(cg * bg, axis=-1, keepdims=True)
        sl = slice(g * gw, (g + 1) * gw)
        y_parts.append(cb * xdt[:, sl] + y_off * dec_exp[:, sl])
        dec_tile = _dot_tn(dec_rows[:, sl], ones_rows)
        outer = _dot_tn(x_rows[:, sl], b_rows)
        hn = dec_tile * h0 + outer
        for k in range(gh):
            hnew_ref[g * gh + k] = hn[k * hd:(k + 1) * hd, :]
    y = jnp.concatenate(y_parts, axis=-1) + dsk_ref[...] * xs
    y_ref[...] = _gated_norm(y, z_ref[...], nw_ref[...], SSM_GROUPS).astype(BF16)


def _ssd_decode(xbc, dt, z, state_conv, state_ssm, layer, lp, *, cfg):
    Bd = xbc.shape[0]
    heads, hd, n_state, inner, cd = cfg["ssm_heads"], cfg["ssm_hd"], cfg["ssm_state"], cfg["ssm_inner"], cfg["conv_dim"]
    kern = functools.partial(_ssd_decode_kernel, heads=heads, hd=hd, n_state=n_state, inner=inner)
    vec = lambda w: pl.BlockSpec((None, 1, w), lambda b: (b, 0, 0))
    y, cnew, hnew = pl.pallas_call(
        kern,
        grid=(Bd,),
        in_specs=[vec(cd),
                  pl.BlockSpec((None, None, CONV_WIDTH - 1, cd), lambda b: (layer, b, 0, 0)),
                  vec(LANES), vec(inner),
                  pl.BlockSpec((None, None, heads, hd, n_state), lambda b: (layer, b, 0, 0, 0)),
                  _whole((CONV_WIDTH, cd)), _whole((1, cd)), _whole((1, LANES)), _whole((1, inner)),
                  _whole((1, inner)), _whole((LANES, inner))],
        out_specs=[vec(inner),
                   pl.BlockSpec((None, CONV_WIDTH - 1, cd), lambda b: (b, 0, 0)),
                   pl.BlockSpec((None, heads, hd, n_state), lambda b: (b, 0, 0, 0))],
        out_shape=[jax.ShapeDtypeStruct((Bd, 1, inner), BF16),
                   jax.ShapeDtypeStruct((Bd, CONV_WIDTH - 1, cd), F32),
                   jax.ShapeDtypeStruct((Bd, heads, hd, n_state), F32)],
        compiler_params=_cparams("parallel"),
        name="ssd_decode",
    )(xbc[:, None, :], state_conv, dt[:, None, :], z[:, None, :], state_ssm, lp["conv_w"], lp["conv_b"], lp["alog"],
      lp["dsk"], lp["nw"], cfg["expand"])
    return y[:, 0, :], cnew, hnew


def _layer_params(l, cfg, ln1, w_in, q_a_norm, w_q_b, kv_a_norm, w_kv_b, conv_w, conv_b, dt_bias, a_log, d_skip,
                  ssm_norm, w_out, ln2, w_gate, w_up, w_down):
    sbw, qr, kvr, rope = cfg["sb_width"], cfg["q_rank"], cfg["kv_rank"], cfg["rope"]
    inner, cd, heads, hd = cfg["ssm_inner"], cfg["conv_dim"], cfg["ssm_heads"], cfg["ssm_hd"]
    w = w_in[l]
    o = 3 * sbw + qr + kvr
    kpe_w = w[:, o:o + rope]
    half = rope // 2
    kpe_sw = jnp.concatenate([kpe_w[:, half:], kpe_w[:, :half]], axis=1)
    rep = LANES // rope
    o2 = o + rope
    dt_w = w[:, o2 + inner + cd:]
    wbig = jnp.concatenate([w[:, :o], jnp.tile(kpe_w, (1, rep)), jnp.tile(kpe_sw, (1, rep)),
                            w[:, o2:o2 + inner + cd], jnp.pad(dt_w, ((0, 0), (0, LANES - heads)))], axis=1)
    H = MLA_HEADS
    wq3 = w_q_b[l].reshape(qr, H, MLA_NOPE + rope)
    pe = wq3[:, :, MLA_NOPE:]
    pe_sw = jnp.concatenate([pe[:, :, half:], pe[:, :, :half]], axis=2)
    wq = jnp.concatenate([wq3[:, :, :MLA_NOPE].reshape(qr, H * MLA_NOPE), pe.reshape(qr, H * rope),
                          pe_sw.reshape(qr, H * rope)], axis=1)
    wkv3 = w_kv_b[l].reshape(kvr, H, MLA_NOPE + MLA_V)
    eye = jnp.eye(H, dtype=F32)
    wuk = jnp.einsum("rhn,hg->hngr", wkv3[:, :, :MLA_NOPE], eye).reshape(H * MLA_NOPE, H * kvr)
    wuv = jnp.einsum("rhv,hg->hrgv", wkv3[:, :, MLA_NOPE:], eye).reshape(H * kvr, H * MLA_V)
    pad_l = lambda x: jnp.pad(x, (0, LANES - x.shape[0]))[None, :]
    return dict(
        ln1=ln1[l][None, :], wbig=wbig.astype(BF16), qan=q_a_norm[l][None, :], wq=wq.astype(BF16),
        wuk=wuk.astype(BF16), wuv=wuv.astype(BF16), kvn=kv_a_norm[l][None, :], dtb=pad_l(dt_bias[l]),
        conv_w=conv_w[l], conv_b=conv_b[l][None, :], alog=pad_l(a_log[l]), dsk=jnp.repeat(d_skip[l], hd)[None, :],
        nw=ssm_norm[l][None, :], wo=w_out[l].astype(BF16), ln2=ln2[l][None, :], wg=w_gate[l].astype(BF16),
        wu=w_up[l].astype(BF16), wd=w_down[l].astype(BF16),
    )


def _config(cache_sb_k, cache_mla_ckv, cache_mla_kpe, state_ssm, q_a_norm):
    sb_heads, sb_hd = cache_sb_k.shape[3], cache_sb_k.shape[4]
    kvr, rope = cache_mla_ckv.shape[3], cache_mla_kpe.shape[3]
    heads, hd, n_state = state_ssm.shape[2], state_ssm.shape[3], state_ssm.shape[4]
    sbw, qr, inner = sb_heads * sb_hd, q_a_norm.shape[1], heads * hd
    cd = inner + 2 * SSM_GROUPS * n_state
    assert LANES % rope == 0 and LANES // rope >= MLA_HEADS and heads <= LANES and heads % (2 * SSM_GROUPS) == 0
    assert 2 * hd == LANES and n_state == LANES and sbw % LANES == 0 and kvr % LANES == 0 and qr % LANES == 0
    segs, o = {}, 0
    for name, wd in (("q_sb", sbw), ("k_sb", sbw), ("v_sb", sbw), ("q_a", qr), ("c_kv", kvr), ("k_pe", LANES),
                     ("k_pe_sw", LANES), ("z", inner), ("xbc", cd), ("dt", LANES)):
        segs[name] = (o, o + wd)
        o += wd
    half = rope // 2
    freq = np.float32(ROPE_THETA) ** (-np.arange(half, dtype=np.float32) / np.float32(half))
    freq = np.tile(freq.astype(np.float32), LANES // half)[None, :]
    expand = np.zeros((LANES, inner), np.float32)
    for h in range(heads):
        expand[h, h * hd:(h + 1) * hd] = 1.0
    return dict(sb_heads=sb_heads, sb_hd=sb_hd, sb_width=sbw, sb_scale=float(sb_hd) ** -0.5, q_rank=qr, kv_rank=kvr,
                rope=rope, mla_scale=float(MLA_NOPE + rope) ** -0.5, ssm_heads=heads, ssm_hd=hd, ssm_state=n_state,
                ssm_inner=inner, conv_dim=cd, segs=segs, na=o, freq=jnp.asarray(freq),
                expand=jnp.asarray(expand, BF16))


def kernel(x_prompt, x_sample, cache_sb_k, cache_sb_v, cache_mla_ckv, cache_mla_kpe, state_conv, state_ssm, page_table, ln1, w_in, q_a_norm, w_q_b, kv_a_norm, w_kv_b, conv_w, conv_b, dt_bias, a_log, d_skip, ssm_norm, w_out, ln2, w_gate, w_up, w_down, final_norm):
    B, S, D = x_prompt.shape
    Bd, Sd, _ = x_sample.shape
    assert Sd == 1, "the decode kernels handle one new token per sequence"
    depth = w_in.shape[0]
    cfg = _config(cache_sb_k, cache_mla_ckv, cache_mla_kpe, state_ssm, q_a_norm)
    assert math.log2(cfg["sb_scale"]).is_integer()
    n_pool, page = cache_sb_k.shape[1], cache_sb_k.shape[2]
    past_len = page_table.shape[1] * page
    token_minor = lambda c: jnp.moveaxis(c, 2, -1)
    caches = (token_minor(cache_sb_k).reshape(depth, n_pool, cfg["sb_width"], page),
              token_minor(cache_sb_v).reshape(depth, n_pool, cfg["sb_width"], page),
              cache_mla_ckv, token_minor(cache_mla_kpe))
    fn = final_norm[None, :]
    lps = [_layer_params(l, cfg, ln1, w_in, q_a_norm, w_q_b, kv_a_norm, w_kv_b, conv_w, conv_b, dt_bias, a_log, d_skip,
                         ssm_norm, w_out, ln2, w_gate, w_up, w_down) for l in range(depth)]

    h = x_prompt.reshape(B * S, D)
    p_state = []
    for l, lp in enumerate(lps):
        (qsb, k32, k16, v32, v16, qlat, qpe, ckv32, kvcat, kpe32, z, xbc, dt) = _in_proj(
            h, lp, seq_len=S, pos_base=0, cfg=cfg)
        osb = _sb_prompt(qsb, k16, v16, B=B, S=S, cfg=cfg)
        olat = _mla_prompt(qlat, qpe, kvcat, B=B, S=S, cfg=cfg)
        yc, conv_new, h_last = _ssd_prompt(xbc, dt, z, lp, B=B, S=S, cfg=cfg)
        h = _out_ffn(h, osb, olat, yc, lp, fn, final=(l == depth - 1), cfg=cfg)
        p_state.append((k32.reshape(B, S, cfg["sb_heads"], cfg["sb_hd"]), v32.reshape(B, S, cfg["sb_heads"], cfg["sb_hd"]),
                        ckv32.reshape(B, S, -1), kpe32.reshape(B, S, -1), conv_new, h_last))
    y_prompt = h.reshape(B, S, D)

    h = x_sample.reshape(Bd, D)
    s_state = []
    for l, lp in enumerate(lps):
        (qsb, k32, k16, v32, v16, qlat, qpe, ckv32, kvcat, kpe32, z, xbc, dt) = _in_proj(
            h, lp, seq_len=Sd, pos_base=past_len, cfg=cfg)
        osb, olat = _dec_attn(qsb, qlat, qpe, kvcat, caches, page_table, l, cfg=cfg)
        yc, conv_new, h_new = _ssd_decode(xbc, dt, z, state_conv, state_ssm, l, lp, cfg=cfg)
        h = _out_ffn(h, osb, olat, yc, lp, fn, final=(l == depth - 1), cfg=cfg)
        s_state.append((k32.reshape(Bd, Sd, cfg["sb_heads"], cfg["sb_hd"]), v32.reshape(Bd, Sd, cfg["sb_heads"], cfg["sb_hd"]),
                        ckv32.reshape(Bd, Sd, -1), kpe32.reshape(Bd, Sd, -1), conv_new, h_new))
    y_sample = h.reshape(Bd, Sd, D)

    stack = lambda states: tuple(jnp.stack(s) for s in zip(*states))
    return (y_prompt, y_sample) + stack(p_state) + stack(s_state)
```

```python
import functools
import math

import numpy as np
import jax
import jax.numpy as jnp
from jax import lax
from jax.experimental import pallas as pl
from jax.experimental.pallas import tpu as pltpu

F32 = jnp.float32
BF16 = jnp.bfloat16

MLA_HEADS = 4
MLA_NOPE = 64
MLA_V = 64
SSM_GROUPS = 2
CONV_WIDTH = 4
SSD_CHUNK = 128
ROPE_THETA = 10000.0
RMS_EPS = 1e-6

LANES = 128
VMEM_LIMIT = 56 * 1024 * 1024


def _cparams(*sem):
    return pltpu.CompilerParams(dimension_semantics=sem, vmem_limit_bytes=VMEM_LIMIT)


def _whole(shape):
    nd = len(shape)
    return pl.BlockSpec(shape, lambda *_: (0,) * nd, pipeline_mode=pl.Buffered(1))


def _dot(a, b):
    return jnp.dot(a, b, preferred_element_type=F32)


def _dot_nt(a, b):
    return lax.dot_general(a, b, (((1,), (1,)), ((), ())), preferred_element_type=F32)


def _dot_tn(a, b):
    return lax.dot_general(a, b, (((0,), (0,)), ((), ())), preferred_element_type=F32)


def _split2(x):
    hi = x.astype(BF16)
    lo = (x - hi.astype(F32)).astype(BF16)
    return hi, lo


def _split3(x):
    hi = x.astype(BF16)
    r = x - hi.astype(F32)
    mid = r.astype(BF16)
    lo = (r - mid.astype(F32)).astype(BF16)
    return hi, mid, lo


def _dot_f32_lhs(x, w):
    a, b, c = _split3(x)
    return _dot(a, w) + _dot(b, w) + _dot(c, w)


def _dot_f32_rhs(w, x):
    a, b, c = _split3(x)
    return _dot(w, a) + _dot(w, b) + _dot(w, c)


def _softplus(x):
    return jnp.maximum(x, 0.0) + jnp.log(1.0 + jnp.exp(-jnp.abs(x)))


def _silu(x):
    return x / (1.0 + jnp.exp(-x))


def _rms(x, w):
    return x * lax.rsqrt(jnp.mean(x * x, axis=-1, keepdims=True) + RMS_EPS) * w


def _in_proj_kernel(h_ref, ln1_ref, w_ref, qan_ref, wq_ref, wuk_ref, kvn_ref, dtb_ref, freq_ref,
                    qsb_ref, k32_ref, k16_ref, v32_ref, v16_ref, qlat_ref, qpe_ref, ckv32_ref, kvcat_ref,
                    kpe32_ref, z_ref, xbc_ref, dt_ref, *, tm, seq_len, pos_base, segs, sb_scale, rope):
    u = _rms(h_ref[...], ln1_ref[...]).astype(BF16)

    def seg(name):
        a, b = segs[name]
        return _dot(u, w_ref[:, a:b])

    qsb_ref[...] = (seg("q_sb") * sb_scale).astype(BF16)
    k = seg("k_sb")
    k32_ref[...] = k
    k16_ref[...] = k.astype(BF16)
    v = seg("v_sb")
    v32_ref[...] = v
    v16_ref[...] = v.astype(BF16)

    row = lax.broadcasted_iota(jnp.int32, (tm, LANES), 0) + pl.program_id(0) * tm
    pos = (pos_base + lax.rem(row, seq_len)).astype(F32)
    ang = pos * freq_ref[...]
    lane = lax.broadcasted_iota(jnp.int32, (tm, LANES), 1)
    second_half = lax.rem(lane, rope) >= rope // 2
    cos_t = jnp.cos(ang)
    sin_t = jnp.sin(ang)
    sin_t = jnp.where(second_half, sin_t, -sin_t)

    qa = _rms(seg("q_a"), qan_ref[...]).astype(BF16)
    q = _dot(qa, wq_ref[...])
    nn = MLA_HEADS * MLA_NOPE
    qlat_ref[...] = _dot(q[:, :nn].astype(BF16), wuk_ref[...]).astype(BF16)
    qpe_ref[...] = (q[:, nn:nn + LANES] * cos_t + q[:, nn + LANES:] * sin_t).astype(BF16)

    ckv = _rms(seg("c_kv"), kvn_ref[...])
    ckv32_ref[...] = ckv
    kpe = seg("k_pe") * cos_t + seg("k_pe_sw") * sin_t
    kpe32_ref[...] = kpe[:, :rope]
    kv_rank = ckv.shape[1]
    kvcat_ref[:, :kv_rank] = ckv.astype(BF16)
    kvcat_ref[:, kv_rank:] = kpe.astype(BF16)

    z_ref[...] = seg("z")
    xbc_ref[...] = seg("xbc")
    dt_ref[...] = _softplus(seg("dt") + dtb_ref[...])


def _in_proj(h, lp, *, seq_len, pos_base, cfg):
    T, D = h.shape
    tm = min(512, T)
    assert T % tm == 0
    segs, NA = cfg["segs"], cfg["na"]
    sbw, kvr, rope = cfg["sb_width"], cfg["kv_rank"], cfg["rope"]
    qlw = MLA_HEADS * kvr
    kern = functools.partial(_in_proj_kernel, tm=tm, seq_len=seq_len, pos_base=pos_base, segs=segs,
                             sb_scale=cfg["sb_scale"], rope=rope)
    row = lambda w: pl.BlockSpec((tm, w), lambda i: (i, 0))
    outs = [
        (sbw, BF16), (sbw, F32), (sbw, BF16), (sbw, F32), (sbw, BF16), (qlw, BF16), (LANES, BF16),
        (kvr, F32), (kvr + LANES, BF16), (rope, F32), (cfg["ssm_inner"], F32), (cfg["conv_dim"], F32), (LANES, F32),
    ]
    return pl.pallas_call(
        kern,
        grid=(T // tm,),
        in_specs=[row(D), _whole((1, D)), _whole((D, NA)), _whole((1, cfg["q_rank"])), _whole(lp["wq"].shape),
                  _whole(lp["wuk"].shape), _whole((1, kvr)), _whole((1, LANES)), _whole((1, LANES))],
        out_specs=[row(w) for w, _ in outs],
        out_shape=[jax.ShapeDtypeStruct((T, w), dt) for w, dt in outs],
        compiler_params=_cparams("parallel"),
        name="in_proj",
    )(h, lp["ln1"], lp["wbig"], lp["qan"], lp["wq"], lp["wuk"], lp["kvn"], lp["dtb"], cfg["freq"])


def _sb_step(qms, k_ref, vm_ref, tt2, carry_ref, acc_ref, s0, r0, valid, tk):
    kb = k_ref[pl.ds(s0, tk), :]
    a = acc_ref[r0:, :]
    for h, qm in enumerate(qms):
        z = _dot_nt(qm[r0:, :], kb)
        lk = -_softplus(z)
        if valid is not None:
            lk = jnp.where(valid, lk, 0.0)
        hi, lo = _split2(lk)
        r = _dot(jnp.concatenate([hi, lo], axis=1), tt2)
        c = carry_ref[h, r0:, :]
        w = jnp.exp(z + r[:, :tk] + c)
        if valid is not None:
            w = jnp.where(valid, w, 0.0)
        carry_ref[h, r0:, :] = c + r[:, tk:]
        a = a + _dot(w.astype(BF16), vm_ref[h, pl.ds(s0, tk), :])
    acc_ref[r0:, :] = a


def _sb_prompt_kernel(q_ref, k_ref, v_ref, tt_ref, o_ref, vm_ref, carry_ref, acc_ref, *, tq, tk, heads, hd, unroll):
    i = pl.program_id(1)
    W = heads * hd
    nb = tq // tk
    lane_head = lax.broadcasted_iota(jnp.int32, (1, W), 1) // hd

    @pl.when(i == 0)
    def _():
        v = v_ref[...]
        for h in range(heads):
            vm_ref[h] = jnp.where(lane_head == h, v, jnp.zeros_like(v))

    q = q_ref[...]
    tt2 = tt_ref[...]
    qms = [jnp.where(lane_head == h, q, jnp.zeros_like(q)) for h in range(heads)]
    acc_ref[...] = jnp.zeros(acc_ref.shape, F32)
    carry_ref[...] = jnp.zeros(carry_ref.shape, F32)

    for d in reversed(range(nb)):
        r0 = d * tk
        rows = lax.broadcasted_iota(jnp.int32, (tq - r0, tk), 0)
        cols = lax.broadcasted_iota(jnp.int32, (tq - r0, tk), 1)
        _sb_step(qms, k_ref, vm_ref, tt2, carry_ref, acc_ref, pl.multiple_of(i * tq + r0, tk), r0, cols < rows, tk)

    def body(jj, _):
        s0 = pl.multiple_of((i * nb - 1 - jj) * tk, tk)
        _sb_step(qms, k_ref, vm_ref, tt2, carry_ref, acc_ref, s0, 0, None, tk)
        return 0

    lax.fori_loop(0, i * nb, body, 0, unroll=unroll)
    o_ref[...] = acc_ref[...].astype(BF16)


SB_TQ, SB_TK, SB_UNROLL = 512, 128, 1


def _sb_prompt(q, k, v, *, B, S, cfg):
    T, W = q.shape
    tk = min(SB_TK, S)
    tq = min(SB_TQ, S)
    assert S % tq == 0 and tq % tk == 0
    heads, hd = cfg["sb_heads"], cfg["sb_hd"]
    nq = S // tq
    kern = functools.partial(_sb_prompt_kernel, tq=tq, tk=tk, heads=heads, hd=hd, unroll=SB_UNROLL)
    return pl.pallas_call(
        kern,
        grid=(B, nq),
        in_specs=[pl.BlockSpec((tq, W), lambda b, i: (b * nq + i, 0)),
                  pl.BlockSpec((S, W), lambda b, i: (b, 0)),
                  pl.BlockSpec((S, W), lambda b, i: (b, 0)),
                  _whole((2 * tk, 2 * tk))],
        out_specs=pl.BlockSpec((tq, W), lambda b, i: (b * nq + i, 0)),
        out_shape=jax.ShapeDtypeStruct((T, W), BF16),
        scratch_shapes=[pltpu.VMEM((heads, S, W), BF16), pltpu.VMEM((heads, tq, tk), F32),
                        pltpu.VMEM((tq, W), F32)],
        compiler_params=_cparams("parallel", "arbitrary"),
        name="sb_prompt",
    )(q, k, v, _rev_cumsum_matrix(tk, stacked=True))


def _rev_cumsum_matrix(tk, stacked=False):
    j = np.arange(tk)
    t = (j[:, None] >= j[None, :]).astype(np.float32)
    m = np.concatenate([t, np.ones((tk, tk), np.float32)], axis=1)
    return jnp.asarray(np.concatenate([m, m], axis=0) if stacked else m, BF16)


def _mla_prompt_kernel(qlat_ref, qpe_ref, kv_ref, o_ref, qcat_ref, vx_ref, m_ref, acc_ref, *, tq, kvr, rope, scale):
    i = pl.program_id(1)
    H = MLA_HEADS
    M = H * tq
    nl = tq // LANES
    na = acc_ref.shape[1] // LANES

    @pl.when(i == 0)
    def _():
        vx_ref[:, :kvr] = kv_ref[:, :kvr]
        vx_ref[:, kvr:] = jnp.ones((vx_ref.shape[0], LANES), BF16)

    qpe = qpe_ref[...]
    pe_head = lax.broadcasted_iota(jnp.int32, (1, LANES), 1) // rope
    for h in range(H):
        qcat_ref[h * tq:(h + 1) * tq, :kvr] = qlat_ref[:, h * kvr:(h + 1) * kvr]
        qcat_ref[h * tq:(h + 1) * tq, kvr:] = jnp.where(pe_head == h, qpe, jnp.zeros_like(qpe))
    m_ref[...] = jnp.full(m_ref.shape, -jnp.inf, F32)
    acc_ref[...] = jnp.zeros(acc_ref.shape, F32)

    def step(s0, masked):
        s = _dot_nt(qcat_ref[...], kv_ref[pl.ds(s0, tq), :]) * scale
        if masked:
            qpos = lax.rem(lax.broadcasted_iota(jnp.int32, (M, tq), 0), tq)
            kpos = lax.broadcasted_iota(jnp.int32, (M, tq), 1)
            s = jnp.where(kpos <= qpos, s, -jnp.inf)
        chunks = [s[:, c * LANES:(c + 1) * LANES] for c in range(nl)]
        cmax = functools.reduce(jnp.maximum, chunks)
        m_old = m_ref[...]
        m_new = jnp.maximum(m_old, jnp.max(cmax, axis=-1, keepdims=True))
        alpha = jnp.exp(m_old - m_new)
        p = jnp.concatenate([jnp.exp(c - m_new) for c in chunks], axis=1).astype(BF16)
        pv = _dot(p, vx_ref[pl.ds(s0, tq), :])
        for c in range(na):
            sl = slice(c * LANES, (c + 1) * LANES)
            acc_ref[:, sl] = alpha * acc_ref[:, sl] + pv[:, sl]
        m_ref[...] = m_new

    def body(j, _):
        step(pl.multiple_of(j * tq, tq), False)
        return 0

    lax.fori_loop(0, i, body, 0)
    step(pl.multiple_of(i * tq, tq), True)
    inv = 1.0 / acc_ref[:, kvr:kvr + LANES]
    for h in range(H):
        for c in range(kvr // LANES):
            o_ref[:, h * kvr + c * LANES:h * kvr + (c + 1) * LANES] = (
                acc_ref[h * tq:(h + 1) * tq, c * LANES:(c + 1) * LANES] * inv[h * tq:(h + 1) * tq]).astype(BF16)


MLA_TQ = 512


def _mla_prompt(qlat, qpe, kvcat, *, B, S, cfg):
    T = qlat.shape[0]
    tq = min(MLA_TQ, S)
    assert S % tq == 0 and tq % LANES == 0
    nq = S // tq
    kvr, rope = cfg["kv_rank"], cfg["rope"]
    kern = functools.partial(_mla_prompt_kernel, tq=tq, kvr=kvr, rope=rope, scale=cfg["mla_scale"])
    H = MLA_HEADS
    return pl.pallas_call(
        kern,
        grid=(B, nq),
        in_specs=[pl.BlockSpec((tq, H * kvr), lambda b, i: (b * nq + i, 0)),
                  pl.BlockSpec((tq, LANES), lambda b, i: (b * nq + i, 0)),
                  pl.BlockSpec((S, kvr + LANES), lambda b, i: (b, 0))],
        out_specs=pl.BlockSpec((tq, H * kvr), lambda b, i: (b * nq + i, 0)),
        out_shape=jax.ShapeDtypeStruct((T, H * kvr), BF16),
        scratch_shapes=[pltpu.VMEM((H * tq, kvr + LANES), BF16), pltpu.VMEM((S, kvr + LANES), BF16),
                        pltpu.VMEM((H * tq, LANES), F32), pltpu.VMEM((H * tq, kvr + LANES), F32)],
        compiler_params=_cparams("parallel", "arbitrary"),
        name="mla_prompt",
    )(qlat, qpe, kvcat)


def _gated_norm(y, z, nw, groups):
    g = y * _silu(z)
    gw = g.shape[-1] // groups
    outs = []
    for k in range(groups):
        gg = g[:, k * gw:(k + 1) * gw]
        outs.append(gg * lax.rsqrt(jnp.mean(gg * gg, axis=-1, keepdims=True) + RMS_EPS))
    return jnp.concatenate(outs, axis=-1) * nw


def _ssd_prompt_kernel(xbc_ref, dt_ref, z_ref, cw_ref, cb_ref, alog_ref, dsk_ref, nw_ref, e_ref, tl_ref,
                       y_ref, conv_ref, hlast_ref, xpad_ref, hs_ref, *, L, heads, hd, n_state, inner):
    c = pl.program_id(1)
    nc = pl.num_programs(1)
    PADR = 8
    gh = heads // SSM_GROUPS
    gw = gh * hd

    @pl.when(c == 0)
    def _():
        xpad_ref[0:PADR, :] = jnp.zeros((PADR, xpad_ref.shape[1]), F32)
        hs_ref[...] = jnp.zeros(hs_ref.shape, F32)

    x_in = xbc_ref[...]
    xpad_ref[PADR:PADR + L, :] = x_in
    conv = cb_ref[...]
    for k in range(CONV_WIDTH - 1):
        off = PADR - (CONV_WIDTH - 1) + k
        conv = conv + cw_ref[k:k + 1, :] * xpad_ref[off:off + L, :]
    conv = conv + cw_ref[CONV_WIDTH - 1:CONV_WIDTH, :] * x_in
    xc = _silu(conv)
    xs = xc[:, :inner]
    bm = xc[:, inner:inner + SSM_GROUPS * n_state]
    cm = xc[:, inner + SSM_GROUPS * n_state:]

    lane = lax.broadcasted_iota(jnp.int32, (1, LANES), 1)
    dt = jnp.where(lane < heads, dt_ref[...], 0.0)
    a = -jnp.exp(alog_ref[...])
    da = dt * a
    tl = tl_ref[...]
    e = e_ref[...]
    cs = _dot_f32_rhs(tl, da)
    cs_t = cs.T
    tot = cs[L - 1:L, :]
    ecs_exp = _dot_f32_lhs(jnp.exp(cs), e)
    dend_exp = _dot_f32_lhs(jnp.exp(tot - cs), e)
    dt_exp = _dot_f32_lhs(dt, e)
    xdt = xs * dt_exp
    xdt16 = xdt.astype(BF16)
    xd16 = (xdt * dend_exp).astype(BF16)

    hs = hs_ref[...]
    hs16 = hs.astype(BF16)
    rows = lax.broadcasted_iota(jnp.int32, (L, L), 0)
    cols = lax.broadcasted_iota(jnp.int32, (L, L), 1)
    tril = cols <= rows
    pair_lane = lax.broadcasted_iota(jnp.int32, (1, 2 * hd), 1) // hd
    etot = jnp.exp(tot)

    y_parts = []
    for g in range(SSM_GROUPS):
        cg = cm[:, g * n_state:(g + 1) * n_state].astype(BF16)
        bg = bm[:, g * n_state:(g + 1) * n_state].astype(BF16)
        scores = _dot_nt(cg, bg)
        y_off = _dot_nt(cg, hs16[g * gw:(g + 1) * gw, :]) * ecs_exp[:, g * gw:(g + 1) * gw]
        pairs = []
        for hp in range(gh // 2):
            h0 = g * gh + 2 * hp
            xpair = xdt16[:, h0 * hd:(h0 + 2) * hd]
            yp = None
            for t in range(2):
                h = h0 + t
                seg = cs[:, h:h + 1] - cs_t[h:h + 1, :]
                dec = jnp.exp(jnp.where(tril, seg, -jnp.inf))
                m = (scores * dec).astype(BF16)
                xm = jnp.where(pair_lane == t, xpair, jnp.zeros_like(xpair))
                d = _dot(m, xm)
                yp = d if yp is None else yp + d
            pairs.append(yp)
        y_parts.append(jnp.concatenate(pairs, axis=-1) + y_off)
        st = _dot_tn(xd16[:, g * gw:(g + 1) * gw], bg)
        for k in range(gh):
            h = g * gh + k
            r0 = h * hd
            hs_ref[r0:r0 + hd, :] = etot[:, h:h + 1] * hs[r0:r0 + hd, :] + st[k * hd:(k + 1) * hd, :]

    y = jnp.concatenate(y_parts, axis=-1) + dsk_ref[...] * xs
    y_ref[...] = _gated_norm(y, z_ref[...], nw_ref[...], SSM_GROUPS).astype(BF16)
    xpad_ref[0:PADR, :] = xpad_ref[L:L + PADR, :]

    @pl.when(c == nc - 1)
    def _():
        conv_ref[0] = xpad_ref[L + PADR - (CONV_WIDTH - 1):L + PADR, :]
        for h in range(heads):
            hlast_ref[0, h] = hs_ref[h * hd:(h + 1) * hd, :]


def _ssd_prompt(xbc, dt, z, lp, *, B, S, cfg):
    T = xbc.shape[0]
    L = min(SSD_CHUNK, S)
    assert S % L == 0 and L >= 8
    nc = S // L
    heads, hd, n_state, inner, cd = cfg["ssm_heads"], cfg["ssm_hd"], cfg["ssm_state"], cfg["ssm_inner"], cfg["conv_dim"]
    kern = functools.partial(_ssd_prompt_kernel, L=L, heads=heads, hd=hd, n_state=n_state, inner=inner)
    row = lambda w: pl.BlockSpec((L, w), lambda b, c: (b * nc + c, 0))
    tl = jnp.asarray(np.tril(np.ones((L, L), np.float32)), BF16)
    return pl.pallas_call(
        kern,
        grid=(B, nc),
        in_specs=[row(cd), row(LANES), row(inner), _whole((CONV_WIDTH, cd)), _whole((1, cd)), _whole((1, LANES)),
                  _whole((1, inner)), _whole((1, inner)), _whole((LANES, inner)), _whole((L, L))],
        out_specs=[row(inner),
                   pl.BlockSpec((1, CONV_WIDTH - 1, cd), lambda b, c: (b, 0, 0)),
                   pl.BlockSpec((1, heads, hd, n_state), lambda b, c: (b, 0, 0, 0))],
        out_shape=[jax.ShapeDtypeStruct((T, inner), BF16),
                   jax.ShapeDtypeStruct((B, CONV_WIDTH - 1, cd), F32),
                   jax.ShapeDtypeStruct((B, heads, hd, n_state), F32)],
        scratch_shapes=[pltpu.VMEM((L + 8, cd), F32), pltpu.VMEM((heads * hd, n_state), F32)],
        compiler_params=_cparams("parallel", "arbitrary"),
        name="ssd_prompt",
    )(xbc, dt, z, lp["conv_w"], lp["conv_b"], lp["alog"], lp["dsk"], lp["nw"], cfg["expand"], tl)


def _out_ffn_kernel(h_ref, osb_ref, olat_ref, yc_ref, wuv_ref, wo_ref, ln2_ref, wg_ref, wu_ref, wd_ref, fn_ref,
                    o_ref, *, sbw, mlaw, hidden, hc, final):
    omla = _dot(olat_ref[...], wuv_ref[...]).astype(BF16)
    mix = (_dot(osb_ref[...], wo_ref[0:sbw, :]) + _dot(omla, wo_ref[sbw:sbw + mlaw, :])
           + _dot(yc_ref[...], wo_ref[sbw + mlaw:, :]))
    h1 = h_ref[...] + mix
    u = _rms(h1, ln2_ref[...]).astype(BF16)
    acc = h1
    for c0 in range(0, hidden, hc):
        g = _dot(u, wg_ref[:, c0:c0 + hc])
        up = _dot(u, wu_ref[:, c0:c0 + hc])
        acc = acc + _dot((_silu(g) * up).astype(BF16), wd_ref[c0:c0 + hc, :])
    o_ref[...] = _rms(acc, fn_ref[...]) if final else acc


def _out_ffn(h, osb, olat, yc, lp, fn, *, final, cfg):
    T, D = h.shape
    tm = min(512, T)
    assert T % tm == 0
    hidden = lp["wg"].shape[1]
    hc = 256 if hidden % 256 == 0 else hidden
    sbw = osb.shape[1]
    mlaw = lp["wuv"].shape[1]
    kern = functools.partial(_out_ffn_kernel, sbw=sbw, mlaw=mlaw, hidden=hidden, hc=hc, final=final)
    row = lambda w: pl.BlockSpec((tm, w), lambda i: (i, 0))
    return pl.pallas_call(
        kern,
        grid=(T // tm,),
        in_specs=[row(D), row(sbw), row(olat.shape[1]), row(yc.shape[1]), _whole(lp["wuv"].shape),
                  _whole(lp["wo"].shape), _whole((1, D)), _whole(lp["wg"].shape), _whole(lp["wu"].shape),
                  _whole(lp["wd"].shape), _whole((1, D))],
        out_specs=row(D),
        out_shape=jax.ShapeDtypeStruct((T, D), F32),
        compiler_params=_cparams("parallel"),
        name="out_ffn",
    )(h, osb, olat, yc, lp["wuv"], lp["wo"], lp["ln2"], lp["wg"], lp["wu"], lp["wd"], fn)


def _page_copies(pt_ref, caches, bufs, sem, layer, b, c, slot, P, NP):
    cps = []
    for p in range(P):
        idx = pt_ref[b, NP - 1 - c * P - p]
        for a, (hbm, buf) in enumerate(zip(caches, bufs)):
            cps.append(pltpu.make_async_copy(hbm.at[layer, idx], buf.at[slot, p], sem.at[slot, a]))
    return cps


def _dec_attn_kernel(pt_ref, qsb_ref, qlat_ref, qpe_ref, kvn_ref, tt_ref, kt_hbm, vt_hbm, c_hbm, pe_hbm,
                     osb_ref, olat_ref, kbuf, vbuf, cbuf, pbuf, sem, qm_ref, ql_ref, qp_ref, carry_ref, accs_ref,
                     m_ref, l_ref, accl_ref, *, layer, P, NP, page, sbw, sb_hd, kvr, rope, scale):
    b = pl.program_id(0)
    c = pl.program_id(1)
    nb = pl.num_programs(0)
    nc = pl.num_programs(1)
    caches, bufs = (kt_hbm, vt_hbm, c_hbm, pe_hbm), (kbuf, vbuf, cbuf, pbuf)
    step = b * nc + c
    slot = lax.rem(step, 2)

    @pl.when(step == 0)
    def _():
        for cp in _page_copies(pt_ref, caches, bufs, sem, layer, b, c, slot, P, NP):
            cp.start()

    @pl.when(step + 1 < nb * nc)
    def _():
        wrap = c + 1 == nc
        for cp in _page_copies(pt_ref, caches, bufs, sem, layer, jnp.where(wrap, b + 1, b),
                               jnp.where(wrap, 0, c + 1), 1 - slot, P, NP):
            cp.start()

    for cp in _page_copies(pt_ref, caches, bufs, sem, layer, b, c, slot, P, NP):
        cp.wait()
    k_refs = [kbuf.at[slot, p] for p in range(P)]
    v_refs = [vbuf.at[slot, p] for p in range(P)]
    c_refs = [cbuf.at[slot, p] for p in range(P)]
    p_refs = [pbuf.at[slot, p] for p in range(P)]
    R = 8
    H = MLA_HEADS

    @pl.when(c == 0)
    def _():
        row_s = lax.broadcasted_iota(jnp.int32, (R, sbw), 0)
        head_s = lax.broadcasted_iota(jnp.int32, (R, sbw), 1) // sb_hd
        q = jnp.broadcast_to(qsb_ref[...].astype(F32), (R, sbw))
        qm_ref[...] = jnp.where(row_s == head_s, q, 0.0).astype(BF16)
        row_l = lax.broadcasted_iota(jnp.int32, (R, kvr), 0)
        row_p = lax.broadcasted_iota(jnp.int32, (R, rope), 0)
        ql = jnp.zeros((R, kvr), F32)
        qp = jnp.zeros((R, rope), F32)
        qlat = qlat_ref[...].astype(F32)
        qpe = qpe_ref[...].astype(F32)
        for h in range(H):
            ql = jnp.where(row_l == h, jnp.broadcast_to(qlat[:, h * kvr:(h + 1) * kvr], (R, kvr)), ql)
            qp = jnp.where(row_p == h, jnp.broadcast_to(qpe[:, h * rope:(h + 1) * rope], (R, rope)), qp)
        ql_ref[...] = ql.astype(BF16)
        qp_ref[...] = qp.astype(BF16)
        kvn = kvn_ref[...].astype(F32)
        ckv_new = kvn[:, :kvr]
        kpe_new = kvn[:, kvr:kvr + rope]
        s_self = (jnp.sum(ql * ckv_new, axis=-1, keepdims=True)
                  + jnp.sum(qp * kpe_new, axis=-1, keepdims=True)) * scale
        m_ref[...] = s_self
        l_ref[...] = jnp.ones((R, 1), F32)
        accl_ref[...] = jnp.broadcast_to(ckv_new, (R, kvr))
        carry_ref[...] = jnp.zeros((R, page), F32)
        accs_ref[...] = jnp.zeros((R, sbw), F32)

    qm = qm_ref[...]
    ql = ql_ref[...]
    qp = qp_ref[...]
    zs = [_dot(qm, k_refs[p][...].astype(BF16)) for p in range(P)]
    cps = [c_refs[p][...].astype(BF16) for p in range(P)]
    ss = [(_dot_nt(ql, cps[p]) + _dot(qp, p_refs[p][...].astype(BF16))) * scale for p in range(P)]

    hi, lo = _split2(jnp.concatenate([-_softplus(z) for z in zs], axis=0))
    r = _dot(jnp.concatenate([hi, lo], axis=1), tt_ref[...])
    carry = carry_ref[...]
    accs = accs_ref[...]
    for p in range(P):
        rp = r[p * R:(p + 1) * R, :]
        w = jnp.exp(zs[p] + rp[:, :page] + carry)
        carry = carry + rp[:, page:]
        accs = accs + _dot_nt(w.astype(BF16), v_refs[p][...].astype(BF16))
    carry_ref[...] = carry
    accs_ref[...] = accs

    m = m_ref[...]
    m_new = jnp.maximum(m, jnp.max(functools.reduce(jnp.maximum, ss), axis=-1, keepdims=True))
    alpha = jnp.exp(m - m_new)
    prs = [jnp.exp(s - m_new) for s in ss]
    l_ref[...] = alpha * l_ref[...] + jnp.sum(functools.reduce(jnp.add, prs), axis=-1, keepdims=True)
    accl = alpha * accl_ref[...]
    for p in range(P):
        accl = accl + _dot(prs[p].astype(BF16), cps[p])
    accl_ref[...] = accl
    m_ref[...] = m_new

    @pl.when(c == nc - 1)
    def _():
        row_s = lax.broadcasted_iota(jnp.int32, (R, sbw), 0)
        head_s = lax.broadcasted_iota(jnp.int32, (R, sbw), 1) // sb_hd
        osb_ref[...] = jnp.sum(jnp.where(row_s == head_s, accs_ref[...], 0.0), axis=0, keepdims=True).astype(BF16)
        o = accl_ref[...] / l_ref[...]
        for h in range(H):
            olat_ref[:, h * kvr:(h + 1) * kvr] = o[h:h + 1, :].astype(BF16)


DEC_PAGES_PER_STEP = 16


def _dec_attn(qsb, qlat, qpe, kvcat, caches, page_table, layer, *, cfg):
    kt4, vt4, c4, pt4 = caches
    Bd, NP = page_table.shape
    page = c4.shape[2]
    sbw, kvr, rope = cfg["sb_width"], cfg["kv_rank"], cfg["rope"]
    P = next(p for p in (DEC_PAGES_PER_STEP, 8, 4, 2, 1) if NP % p == 0)
    nc = NP // P

    vec = lambda w: pl.BlockSpec((None, 1, w), lambda b, c, pt: (b, 0, 0))
    hbm = pl.BlockSpec(memory_space=pl.ANY)
    in_specs = [vec(sbw), vec(MLA_HEADS * kvr), vec(LANES), vec(kvr + LANES),
                pl.BlockSpec((2 * page, 2 * page), lambda b, c, pt: (0, 0)), hbm, hbm, hbm, hbm]
    args = [qsb[:, None, :], qlat[:, None, :], qpe[:, None, :], kvcat[:, None, :],
            _rev_cumsum_matrix(page, stacked=True), kt4, vt4, c4, pt4]
    kern = functools.partial(_dec_attn_kernel, layer=layer, P=P, NP=NP, page=page, sbw=sbw, sb_hd=cfg["sb_hd"],
                             kvr=kvr, rope=rope, scale=cfg["mla_scale"])
    R = 8
    n_slots = 2
    osb, olat = pl.pallas_call(
        kern,
        grid_spec=pltpu.PrefetchScalarGridSpec(
            num_scalar_prefetch=1,
            grid=(Bd, nc),
            in_specs=in_specs,
            out_specs=[vec(sbw), vec(MLA_HEADS * kvr)],
            scratch_shapes=[pltpu.VMEM((n_slots, P, sbw, page), F32), pltpu.VMEM((n_slots, P, sbw, page), F32),
                            pltpu.VMEM((n_slots, P, page, kvr), F32), pltpu.VMEM((n_slots, P, rope, page), F32),
                            pltpu.SemaphoreType.DMA((n_slots, 4)),
                            pltpu.VMEM((R, sbw), BF16), pltpu.VMEM((R, kvr), BF16), pltpu.VMEM((R, rope), BF16),
                            pltpu.VMEM((R, page), F32), pltpu.VMEM((R, sbw), F32), pltpu.VMEM((R, 1), F32),
                            pltpu.VMEM((R, 1), F32), pltpu.VMEM((R, kvr), F32)],
        ),
        out_shape=[jax.ShapeDtypeStruct((Bd, 1, sbw), BF16), jax.ShapeDtypeStruct((Bd, 1, MLA_HEADS * kvr), BF16)],
        compiler_params=_cparams("arbitrary", "arbitrary"),
        name="dec_attn",
    )(page_table, *args)
    return osb[:, 0, :], olat[:, 0, :]


def _ssd_decode_kernel(xbc_ref, cst_ref, dt_ref, z_ref, h_ref, cw_ref, cb_ref, alog_ref, dsk_ref, nw_ref, e_ref,
                       y_ref, cnew_ref, hnew_ref, *, heads, hd, n_state, inner):
    R = 8
    gh = heads // SSM_GROUPS
    gw = gh * hd
    xr = xbc_ref[...]
    st = cst_ref[...]
    conv = cb_ref[...]
    for k in range(CONV_WIDTH - 1):
        conv = conv + cw_ref[k:k + 1, :] * st[k:k + 1, :]
    conv = conv + cw_ref[CONV_WIDTH - 1:CONV_WIDTH, :] * xr
    for k in range(CONV_WIDTH - 2):
        cnew_ref[k:k + 1, :] = st[k + 1:k + 2, :]
    cnew_ref[CONV_WIDTH - 2:CONV_WIDTH - 1, :] = xr
    xc = _silu(conv)
    xs = xc[:, :inner]
    bm = xc[:, inner:inner + SSM_GROUPS * n_state]
    cm = xc[:, inner + SSM_GROUPS * n_state:]

    lane = lax.broadcasted_iota(jnp.int32, (1, LANES), 1)
    dt = jnp.where(lane < heads, dt_ref[...], 0.0)
    da = dt * (-jnp.exp(alog_ref[...]))
    e = e_ref[...]
    row8 = lambda x: jnp.broadcast_to(x, (R, x.shape[1]))
    dt_exp = _dot_f32_lhs(row8(dt), e)[0:1, :]
    dec_exp = _dot_f32_lhs(row8(jnp.exp(da)), e)[0:1, :]
    xdt = xs * dt_exp

    rid = lambda w: lax.broadcasted_iota(jnp.int32, (R, w), 0)
    first = lambda x: jnp.where(rid(x.shape[1]) == 0, row8(x), 0.0).astype(BF16)
    d_hi = dec_exp.astype(BF16).astype(F32)
    d_mid = (dec_exp - d_hi).astype(BF16).astype(F32)
    d_lo = dec_exp - d_hi - d_mid
    dec_rows = jnp.zeros((R, inner), F32)
    for k, piece in enumerate((d_hi, d_mid, d_lo)):
        dec_rows = jnp.where(rid(inner) == k, row8(piece), dec_rows)
    dec_rows = dec_rows.astype(BF16)
    ones_rows = jnp.where(rid(n_state) < 3, 1.0, 0.0).astype(BF16)
    x_rows = first(xdt)

    y_parts = []
    for g in range(SSM_GROUPS):
        cg = cm[:, g * n_state:(g + 1) * n_state]
        bg = bm[:, g * n_state:(g + 1) * n_state]
        c_rows = first(cg)
        b_rows = first(bg)
        h0 = jnp.concatenate([h_ref[g * gh + k] for k in range(gh)], axis=0)
        y_off = _dot_nt(c_rows, h0.astype(BF16))[0:1, :]
        cb = jnp.sum(cg * bg, axis=-1, keepdims=True)
        sl = slice(g * gw, (g + 1) * gw)
        y_parts.append(cb * xdt[:, sl] + y_off * dec_exp[:, sl])
        dec_tile = _dot_tn(dec_rows[:, sl], ones_rows)
        outer = _dot_tn(x_rows[:, sl], b_rows)
        hn = dec_tile * h0 + outer
        for k in range(gh):
            hnew_ref[g * gh + k] = hn[k * hd:(k + 1) * hd, :]
    y = jnp.concatenate(y_parts, axis=-1) + dsk_ref[...] * xs
    y_ref[...] = _gated_norm(y, z_ref[...], nw_ref[...], SSM_GROUPS).astype(BF16)


def _ssd_decode(xbc, dt, z, state_conv, state_ssm, layer, lp, *, cfg):
    Bd = xbc.shape[0]
    heads, hd, n_state, inner, cd = cfg["ssm_heads"], cfg["ssm_hd"], cfg["ssm_state"], cfg["ssm_inner"], cfg["conv_dim"]
    kern = functools.partial(_ssd_decode_kernel, heads=heads, hd=hd, n_state=n_state, inner=inner)
    vec = lambda w: pl.BlockSpec((None, 1, w), lambda b: (b, 0, 0))
    y, cnew, hnew = pl.pallas_call(
        kern,
        grid=(Bd,),
        in_specs=[vec(cd),
                  pl.BlockSpec((None, None, CONV_WIDTH - 1, cd), lambda b: (layer, b, 0, 0)),
                  vec(LANES), vec(inner),
                  pl.BlockSpec((None, None, heads, hd, n_state), lambda b: (layer, b, 0, 0, 0)),
                  _whole((CONV_WIDTH, cd)), _whole((1, cd)), _whole((1, LANES)), _whole((1, inner)),
                  _whole((1, inner)), _whole((LANES, inner))],
        out_specs=[vec(inner),
                   pl.BlockSpec((None, CONV_WIDTH - 1, cd), lambda b: (b, 0, 0)),
                   pl.BlockSpec((None, heads, hd, n_state), lambda b: (b, 0, 0, 0))],
        out_shape=[jax.ShapeDtypeStruct((Bd, 1, inner), BF16),
                   jax.ShapeDtypeStruct((Bd, CONV_WIDTH - 1, cd), F32),
                   jax.ShapeDtypeStruct((Bd, heads, hd, n_state), F32)],
        compiler_params=_cparams("parallel"),
        name="ssd_decode",
    )(xbc[:, None, :], state_conv, dt[:, None, :], z[:, None, :], state_ssm, lp["conv_w"], lp["conv_b"], lp["alog"],
      lp["dsk"], lp["nw"], cfg["expand"])
    return y[:, 0, :], cnew, hnew


def _layer_params(l, cfg, ln1, w_in, q_a_norm, w_q_b, kv_a_norm, w_kv_b, conv_w, conv_b, dt_bias, a_log, d_skip,
                  ssm_norm, w_out, ln2, w_gate, w_up, w_down):
    sbw, qr, kvr, rope = cfg["sb_width"], cfg["q_rank"], cfg["kv_rank"], cfg["rope"]
    inner, cd, heads, hd = cfg["ssm_inner"], cfg["conv_dim"], cfg["ssm_heads"], cfg["ssm_hd"]
    w = w_in[l]
    o = 3 * sbw + qr + kvr
    kpe_w = w[:, o:o + rope]
    half = rope // 2
    kpe_sw = jnp.concatenate([kpe_w[:, half:], kpe_w[:, :half]], axis=1)
    rep = LANES // rope
    o2 = o + rope
    dt_w = w[:, o2 + inner + cd:]
    wbig = jnp.concatenate([w[:, :o], jnp.tile(kpe_w, (1, rep)), jnp.tile(kpe_sw, (1, rep)),
                            w[:, o2:o2 + inner + cd], jnp.pad(dt_w, ((0, 0), (0, LANES - heads)))], axis=1)
    H = MLA_HEADS
    wq3 = w_q_b[l].reshape(qr, H, MLA_NOPE + rope)
    pe = wq3[:, :, MLA_NOPE:]
    pe_sw = jnp.concatenate([pe[:, :, half:], pe[:, :, :half]], axis=2)
    wq = jnp.concatenate([wq3[:, :, :MLA_NOPE].reshape(qr, H * MLA_NOPE), pe.reshape(qr, H * rope),
                          pe_sw.reshape(qr, H * rope)], axis=1)
    wkv3 = w_kv_b[l].reshape(kvr, H, MLA_NOPE + MLA_V)
    eye = jnp.eye(H, dtype=F32)
    wuk = jnp.einsum("rhn,hg->hngr", wkv3[:, :, :MLA_NOPE], eye).reshape(H * MLA_NOPE, H * kvr)
    wuv = jnp.einsum("rhv,hg->hrgv", wkv3[:, :, MLA_NOPE:], eye).reshape(H * kvr, H * MLA_V)
    pad_l = lambda x: jnp.pad(x, (0, LANES - x.shape[0]))[None, :]
    return dict(
        ln1=ln1[l][None, :], wbig=wbig.astype(BF16), qan=q_a_norm[l][None, :], wq=wq.astype(BF16),
        wuk=wuk.astype(BF16), wuv=wuv.astype(BF16), kvn=kv_a_norm[l][None, :], dtb=pad_l(dt_bias[l]),
        conv_w=conv_w[l], conv_b=conv_b[l][None, :], alog=pad_l(a_log[l]), dsk=jnp.repeat(d_skip[l], hd)[None, :],
        nw=ssm_norm[l][None, :], wo=w_out[l].astype(BF16), ln2=ln2[l][None, :], wg=w_gate[l].astype(BF16),
        wu=w_up[l].astype(BF16), wd=w_down[l].astype(BF16),
    )


def _config(cache_sb_k, cache_mla_ckv, cache_mla_kpe, state_ssm, q_a_norm):
    sb_heads, sb_hd = cache_sb_k.shape[3], cache_sb_k.shape[4]
    kvr, rope = cache_mla_ckv.shape[3], cache_mla_kpe.shape[3]
    heads, hd, n_state = state_ssm.shape[2], state_ssm.shape[3], state_ssm.shape[4]
    sbw, qr, inner = sb_heads * sb_hd, q_a_norm.shape[1], heads * hd
    cd = inner + 2 * SSM_GROUPS * n_state
    assert LANES % rope == 0 and LANES // rope >= MLA_HEADS and heads <= LANES and heads % (2 * SSM_GROUPS) == 0
    assert 2 * hd == LANES and n_state == LANES and sbw % LANES == 0 and kvr % LANES == 0 and qr % LANES == 0
    segs, o = {}, 0
    for name, wd in (("q_sb", sbw), ("k_sb", sbw), ("v_sb", sbw), ("q_a", qr), ("c_kv", kvr), ("k_pe", LANES),
                     ("k_pe_sw", LANES), ("z", inner), ("xbc", cd), ("dt", LANES)):
        segs[name] = (o, o + wd)
        o += wd
    half = rope // 2
    freq = np.float32(ROPE_THETA) ** (-np.arange(half, dtype=np.float32) / np.float32(half))
    freq = np.tile(freq.astype(np.float32), LANES // half)[None, :]
    expand = np.zeros((LANES, inner), np.float32)
    for h in range(heads):
        expand[h, h * hd:(h + 1) * hd] = 1.0
    return dict(sb_heads=sb_heads, sb_hd=sb_hd, sb_width=sbw, sb_scale=float(sb_hd) ** -0.5, q_rank=qr, kv_rank=kvr,
                rope=rope, mla_scale=float(MLA_NOPE + rope) ** -0.5, ssm_heads=heads, ssm_hd=hd, ssm_state=n_state,
                ssm_inner=inner, conv_dim=cd, segs=segs, na=o, freq=jnp.asarray(freq),
                expand=jnp.asarray(expand, BF16))


def kernel(x_prompt, x_sample, cache_sb_k, cache_sb_v, cache_mla_ckv, cache_mla_kpe, state_conv, state_ssm, page_table, ln1, w_in, q_a_norm, w_q_b, kv_a_norm, w_kv_b, conv_w, conv_b, dt_bias, a_log, d_skip, ssm_norm, w_out, ln2, w_gate, w_up, w_down, final_norm):
    B, S, D = x_prompt.shape
    Bd, Sd, _ = x_sample.shape
    assert Sd == 1, "the decode kernels handle one new token per sequence"
    depth = w_in.shape[0]
    cfg = _config(cache_sb_k, cache_mla_ckv, cache_mla_kpe, state_ssm, q_a_norm)
    assert math.log2(cfg["sb_scale"]).is_integer()
    n_pool, page = cache_sb_k.shape[1], cache_sb_k.shape[2]
    past_len = page_table.shape[1] * page
    token_minor = lambda c: jnp.moveaxis(c, 2, -1)
    caches = (token_minor(cache_sb_k).reshape(depth, n_pool, cfg["sb_width"], page),
              token_minor(cache_sb_v).reshape(depth, n_pool, cfg["sb_width"], page),
              cache_mla_ckv, token_minor(cache_mla_kpe))
    fn = final_norm[None, :]
    lps = [_layer_params(l, cfg, ln1, w_in, q_a_norm, w_q_b, kv_a_norm, w_kv_b, conv_w, conv_b, dt_bias, a_log, d_skip,
                         ssm_norm, w_out, ln2, w_gate, w_up, w_down) for l in range(depth)]

    h = x_prompt.reshape(B * S, D)
    p_state = []
    for l, lp in enumerate(lps):
        (qsb, k32, k16, v32, v16, qlat, qpe, ckv32, kvcat, kpe32, z, xbc, dt) = _in_proj(
            h, lp, seq_len=S, pos_base=0, cfg=cfg)
        osb = _sb_prompt(qsb, k16, v16, B=B, S=S, cfg=cfg)
        olat = _mla_prompt(qlat, qpe, kvcat, B=B, S=S, cfg=cfg)
        yc, conv_new, h_last = _ssd_prompt(xbc, dt, z, lp, B=B, S=S, cfg=cfg)
        h = _out_ffn(h, osb, olat, yc, lp, fn, final=(l == depth - 1), cfg=cfg)
        p_state.append((k32.reshape(B, S, cfg["sb_heads"], cfg["sb_hd"]), v32.reshape(B, S, cfg["sb_heads"], cfg["sb_hd"]),
                        ckv32.reshape(B, S, -1), kpe32.reshape(B, S, -1), conv_new, h_last))
    y_prompt = h.reshape(B, S, D)

    h = x_sample.reshape(Bd, D)
    s_state = []
    for l, lp in enumerate(lps):
        (qsb, k32, k16, v32, v16, qlat, qpe, ckv32, kvcat, kpe32, z, xbc, dt) = _in_proj(
            h, lp, seq_len=Sd, pos_base=past_len, cfg=cfg)
        osb, olat = _dec_attn(qsb, qlat, qpe, kvcat, caches, page_table, l, cfg=cfg)
        yc, conv_new, h_new = _ssd_decode(xbc, dt, z, state_conv, state_ssm, l, lp, cfg=cfg)
        h = _out_ffn(h, osb, olat, yc, lp, fn, final=(l == depth - 1), cfg=cfg)
        s_state.append((k32.reshape(Bd, Sd, cfg["sb_heads"], cfg["sb_hd"]), v32.reshape(Bd, Sd, cfg["sb_heads"], cfg["sb_hd"]),
                        ckv32.reshape(Bd, Sd, -1), kpe32.reshape(Bd, Sd, -1), conv_new, h_new))
    y_sample = h.reshape(Bd, Sd, D)

    stack = lambda states: tuple(jnp.stack(s) for s in zip(*states))
    return (y_prompt, y_sample) + stack(p_state) + stack(s_state)
```
